```python
import jax
import jax.numpy as jnp
from jax import lax
import numpy as np

D_MODEL = 1024
BATCH = 8
SEQ = 4096
DEPTH = 1
DEC_BATCH = 128
DEC_SEQ = 1
PAST_LEN = 16384
PAGE_SIZE = 128

MLA_HEADS = 8
MLA_NOPE = 64
MLA_ROPE = 32
MLA_QK = MLA_NOPE + MLA_ROPE
MLA_V = 64
Q_LORA = 768
KV_LORA = 256
ROPE_THETA = 10000.0
MLA_SCALE = MLA_QK ** -0.5
SB_HEADS = 8
SB_DIM = 64
SB_WIDTH = SB_HEADS * SB_DIM
SB_SCALE = SB_DIM ** -0.5
MIX_WIDTH = MLA_HEADS * MLA_V + SB_WIDTH
IN_SPLITS = (Q_LORA, Q_LORA + KV_LORA, Q_LORA + KV_LORA + MLA_ROPE,
             Q_LORA + KV_LORA + MLA_ROPE + SB_WIDTH, Q_LORA + KV_LORA + MLA_ROPE + 2 * SB_WIDTH)
IN_WIDTH = Q_LORA + KV_LORA + MLA_ROPE + 3 * SB_WIDTH
N_EXPERTS = 256
TOP_K = 8
N_GROUPS = 8
TOPK_GROUPS = 4
D_EXPERT = 256
D_SHARED = 256
ROUTE_SCALE = 2.5
MOE_BLOCK = 128
Q_BLOCK = 128
EPS = 1e-6

kernel_name = 'hymba_mla_stickbreak_moe_adaln_step'


def rms_norm(x, g):
    xf = x.astype(jnp.float32)
    y = xf * lax.rsqrt(jnp.mean(xf * xf, axis=-1, keepdims=True) + EPS)
    return (y * g.astype(jnp.float32)).astype(x.dtype)


def rope_angles(pos):
    half = MLA_ROPE // 2
    inv_freq = ROPE_THETA ** (-jnp.arange(half, dtype=jnp.float32) * (2.0 / MLA_ROPE))
    ang = pos.astype(jnp.float32)[:, None] * inv_freq[None, :]
    return jnp.cos(ang), jnp.sin(ang)


def apply_rope(x, cos, sin):
    half = x.shape[-1] // 2
    x1 = x[..., :half].astype(jnp.float32)
    x2 = x[..., half:].astype(jnp.float32)
    return jnp.concatenate([x1 * cos - x2 * sin, x2 * cos + x1 * sin], axis=-1).astype(x.dtype)


def adaln(c, lw):
    m = jax.nn.silu(c) @ lw['w_ada'] + lw['b_ada']
    return jnp.split(m[:, None, :], 6, axis=-1)


def modulate(h, shift, scale):
    return h * (1.0 + scale) + shift


def stick_breaking_weights(z, mask):
    z = z.astype(jnp.float32)
    log_fail = jnp.where(mask, -jax.nn.softplus(z), 0.0)
    later = lax.cumsum(log_fail, axis=z.ndim - 1, reverse=True) - log_fail
    return jnp.where(mask, jnp.exp(jax.nn.log_sigmoid(z) + later), 0.0)


def mixer_projections(h, pos, lw):
    b, s, _ = h.shape
    cos, sin = rope_angles(pos)
    q_lat, kv_lat, k_rope_raw, sb_q, sb_k, sb_v = jnp.split(h @ lw['w_in'], IN_SPLITS, axis=-1)
    q = (rms_norm(q_lat, lw['g_q_lat']) @ lw['w_uq']).reshape(b, s, MLA_HEADS, MLA_QK)
    q = rms_norm(q, lw['g_qk_q'])
    g_k = lw['g_qk_k']
    q_nope = q[..., :MLA_NOPE] * g_k[:MLA_NOPE]
    q_rope = apply_rope(q[..., MLA_NOPE:], cos[:, None, :], sin[:, None, :])
    c_kv = rms_norm(kv_lat, lw['g_kv_lat'])
    k_nope = (c_kv @ lw['w_uk']).reshape(b, s, MLA_HEADS, MLA_NOPE)
    kn = k_nope.astype(jnp.float32)
    kr = k_rope_raw.astype(jnp.float32)
    k_inv = lax.rsqrt((jnp.sum(kn * kn, axis=-1) + jnp.sum(kr * kr, axis=-1, keepdims=True)) / MLA_QK + EPS)
    k_pe = apply_rope(k_rope_raw * g_k[MLA_NOPE:], cos, sin)
    return (q_nope, q_rope, c_kv, k_nope, k_pe, k_inv,
            sb_q.reshape(b, s, SB_HEADS, SB_DIM), sb_k.reshape(b, s, SB_HEADS, SB_DIM),
            sb_v.reshape(b, s, SB_HEADS, SB_DIM))


def mla_prompt(q_nope, q_rope, k_nope, k_pe, k_inv, v):
    b, s, h, _ = q_nope.shape
    nb = s // Q_BLOCK
    qn = q_nope.reshape(b, nb, Q_BLOCK, h, MLA_NOPE).swapaxes(0, 1)
    qr = q_rope.reshape(b, nb, Q_BLOCK, h, MLA_ROPE).swapaxes(0, 1)
    kinv_t = k_inv.transpose(0, 2, 1)[:, :, None, :]
    kpos = jnp.arange(s)

    def block(args):
        qn_b, qr_b, start = args
        sc = jnp.einsum('bqhn,bkhn->bhqk', qn_b, k_nope) + jnp.einsum('bqhr,bkr->bhqk', qr_b, k_pe)
        sc = sc.astype(jnp.float32) * kinv_t * MLA_SCALE
        qpos = start + jnp.arange(Q_BLOCK)
        sc = jnp.where(kpos[None, :] <= qpos[:, None], sc, -jnp.inf)
        p = jax.nn.softmax(sc, axis=-1)
        return jnp.einsum('bhqk,bkhd->bqhd', p.astype(v.dtype), v)

    o = lax.map(block, (qn, qr, jnp.arange(nb) * Q_BLOCK))
    return o.swapaxes(0, 1).reshape(b, s, h * MLA_V)


def sb_prompt(q, k, v):
    b, s, h, d = q.shape
    nb = s // Q_BLOCK
    qb = q.reshape(b, nb, Q_BLOCK, h, d).swapaxes(0, 1)
    kpos = jnp.arange(s)

    def block(args):
        q_b, start = args
        z = jnp.einsum('bqhd,bkhd->bhqk', q_b, k) * SB_SCALE
        qpos = start + jnp.arange(Q_BLOCK)
        a = stick_breaking_weights(z, kpos[None, :] < qpos[:, None])
        return jnp.einsum('bhqk,bkhd->bqhd', a.astype(v.dtype), v)

    o = lax.map(block, (qb, jnp.arange(nb) * Q_BLOCK))
    return o.swapaxes(0, 1).reshape(b, s, h * d)


def sample_attention(l, page_table, cache_mla_ckv, cache_mla_kpe, cache_mla_kinv, cache_sb_k, cache_sb_v,
                     q_abs, q_rope, c_kv, k_pe, k_inv, sb_q, sb_k, sb_v):
    past = page_table.shape[1] * PAGE_SIZE
    n = q_abs.shape[1]
    qpos = past + jnp.arange(n)
    kpos = jnp.arange(past + n)
    causal = kpos[None, :] <= qpos[:, None]
    strict = kpos[None, :] < qpos[:, None]

    def one(args):
        pages, qa, qr, c_new, pe_new, inv_new, sq, sk, sv = args
        ckv = jnp.concatenate([cache_mla_ckv[l, pages].reshape(past, KV_LORA), c_new], axis=0)
        kpe = jnp.concatenate([cache_mla_kpe[l, pages].reshape(past, MLA_ROPE), pe_new], axis=0)
        kinv = jnp.concatenate([cache_mla_kinv[l, pages].reshape(past, MLA_HEADS), inv_new], axis=0)
        sc = jnp.einsum('qhc,kc->hqk', qa, ckv) + jnp.einsum('qhr,kr->hqk', qr, kpe)
        sc = sc.astype(jnp.float32) * kinv.astype(jnp.float32).T[:, None, :] * MLA_SCALE
        p = jax.nn.softmax(jnp.where(causal, sc, -jnp.inf), axis=-1)
        o_lat = jnp.einsum('hqk,kc->qhc', p.astype(ckv.dtype), ckv)
        kk = jnp.concatenate([cache_sb_k[l, pages].reshape(past, SB_HEADS, SB_DIM), sk], axis=0)
        vv = jnp.concatenate([cache_sb_v[l, pages].reshape(past, SB_HEADS, SB_DIM), sv], axis=0)
        z = jnp.einsum('qhd,khd->hqk', sq, kk) * SB_SCALE
        a = stick_breaking_weights(z, strict)
        o_sb = jnp.einsum('hqk,khd->qhd', a.astype(vv.dtype), vv)
        return o_lat, o_sb

    return lax.map(one, (page_table, q_abs, q_rope, c_kv, k_pe, k_inv, sb_q, sb_k, sb_v))


def swiglu(x, wg, wu, wd):
    return (jax.nn.silu(x @ wg) * (x @ wu)) @ wd


def route(t, w_router, b_router):
    n_tok = t.shape[0]
    scores = jax.nn.sigmoid((t @ w_router).astype(jnp.float32))
    sel = scores + b_router.astype(jnp.float32)
    grp = sel.reshape(n_tok, N_GROUPS, N_EXPERTS // N_GROUPS)
    grp_score = lax.top_k(grp, 2)[0].sum(axis=-1)
    _, top_groups = lax.top_k(grp_score, TOPK_GROUPS)
    gmask = jax.nn.one_hot(top_groups, N_GROUPS, dtype=jnp.float32).sum(axis=-2) > 0
    emask = jnp.repeat(gmask, N_EXPERTS // N_GROUPS, axis=-1)
    _, idx = lax.top_k(jnp.where(emask, sel, -jnp.inf), TOP_K)
    w = jnp.take_along_axis(scores, idx, axis=-1)
    w = w / jnp.sum(w, axis=-1, keepdims=True) * ROUTE_SCALE
    return idx, w


def routed_experts(x, idx, w, w_gate, w_up, w_down):
    n_tok, d = x.shape
    n = n_tok * TOP_K
    blk = MOE_BLOCK if n >= N_EXPERTS * MOE_BLOCK else 8
    flat_e = idx.reshape(-1)
    flat_tok = jnp.arange(n) // TOP_K
    flat_w = w.reshape(-1)
    order = jnp.argsort(flat_e)
    se = flat_e[order]
    counts = jnp.bincount(flat_e, length=N_EXPERTS)
    padded = (counts + blk - 1) // blk * blk
    start = jnp.cumsum(counts) - counts
    pend = jnp.cumsum(padded)
    pstart = pend - padded
    dest = pstart[se] + jnp.arange(n) - start[se]
    n_rows = -(-n // blk) * blk + N_EXPERTS * blk
    n_blocks = n_rows // blk
    row_tok = jnp.full((n_rows,), n_tok, jnp.int32).at[dest].set(flat_tok[order].astype(jnp.int32))
    row_w = jnp.zeros((n_rows,), jnp.float32).at[dest].set(flat_w[order])
    block_e = jnp.minimum(jnp.searchsorted(pend, jnp.arange(n_blocks) * blk, side='right'), N_EXPERTS - 1)
    xp = jnp.concatenate([x, jnp.zeros((1, d), x.dtype)], axis=0)

    def run(args):
        tb, e = args
        return swiglu(xp[tb], w_gate[e], w_up[e], w_down[e])

    ys = lax.map(run, (row_tok.reshape(n_blocks, blk), block_e)).reshape(n_rows, d)
    y = jnp.zeros((n_tok + 1, d), ys.dtype).at[row_tok].add(ys * row_w[:, None].astype(ys.dtype))
    return y[:n_tok]


def moe_sublayer(h, lw):
    b, s, d = h.shape
    t = h.reshape(b * s, d)
    idx, w = route(t, lw['w_router'], lw['b_router'])
    y = swiglu(t, lw['w_sh_gate'], lw['w_sh_up'], lw['w_sh_down']) + routed_experts(
        t, idx, w, lw['w_exp_gate'], lw['w_exp_up'], lw['w_exp_down'])
    return y.reshape(b, s, d)


def layer_prompt(x, c, lw):
    b, s, _ = x.shape
    sh_a, sc_a, g_a, sh_m, sc_m, g_m = adaln(c, lw)
    h = modulate(rms_norm(x, lw['g_attn']), sh_a, sc_a)
    q_nope, q_rope, c_kv, k_nope, k_pe, k_inv, sb_q, sb_k, sb_v = mixer_projections(h, jnp.arange(s), lw)
    v_mla = (c_kv @ lw['w_uv']).reshape(b, s, MLA_HEADS, MLA_V)
    o_mla = mla_prompt(q_nope, q_rope, k_nope, k_pe, k_inv, v_mla)
    o_sb = sb_prompt(sb_q, sb_k, sb_v)
    x = x + g_a * (jnp.concatenate([o_mla, o_sb], axis=-1) @ lw['w_o'])
    x = x + g_m * moe_sublayer(modulate(rms_norm(x, lw['g_moe']), sh_m, sc_m), lw)
    return x, (c_kv, k_pe, k_inv, sb_k, sb_v)


def layer_sample(x, c, l, page_table, cache_mla_ckv, cache_mla_kpe, cache_mla_kinv, cache_sb_k, cache_sb_v, lw):
    b, n, _ = x.shape
    past = page_table.shape[1] * PAGE_SIZE
    sh_a, sc_a, g_a, sh_m, sc_m, g_m = adaln(c, lw)
    h = modulate(rms_norm(x, lw['g_attn']), sh_a, sc_a)
    q_nope, q_rope, c_kv, _, k_pe, k_inv, sb_q, sb_k, sb_v = mixer_projections(h, past + jnp.arange(n), lw)
    q_abs = jnp.einsum('bshn,chn->bshc', q_nope, lw['w_uk'].reshape(KV_LORA, MLA_HEADS, MLA_NOPE))
    o_lat, o_sb = sample_attention(l, page_table, cache_mla_ckv, cache_mla_kpe, cache_mla_kinv, cache_sb_k,
                                   cache_sb_v, q_abs, q_rope, c_kv, k_pe, k_inv, sb_q, sb_k, sb_v)
    o_mla = jnp.einsum('bshc,chd->bshd', o_lat, lw['w_uv'].reshape(KV_LORA, MLA_HEADS, MLA_V))
    o = jnp.concatenate([o_mla.reshape(b, n, MLA_HEADS * MLA_V), o_sb.reshape(b, n, SB_WIDTH)], axis=-1)
    x = x + g_a * (o @ lw['w_o'])
    x = x + g_m * moe_sublayer(modulate(rms_norm(x, lw['g_moe']), sh_m, sc_m), lw)
    return x, (c_kv, k_pe, k_inv, sb_k, sb_v)


def setup_inputs(seed: int = 0) -> dict:
    key = jax.random.key(seed)
    ks = jax.random.split(key, 40)
    f32 = jnp.float32
    n_pages = PAST_LEN // PAGE_SIZE
    n_used = DEC_BATCH * n_pages
    n_pool = n_used + n_used // 4 + 1

    def nrm(k, shape, s):
        return jax.random.normal(k, shape, f32) * s

    def gain(k, shape):
        return 1.0 + 0.05 * jax.random.normal(k, shape, f32)

    page_table = jax.random.permutation(ks[7], n_pool)[:n_used].reshape(DEC_BATCH, n_pages).astype(jnp.int32)
    return {
        'x_prompt': nrm(ks[0], (BATCH, SEQ, D_MODEL), 1.0),
        'x_sample': nrm(ks[1], (DEC_BATCH, DEC_SEQ, D_MODEL), 1.0),
        'cache_mla_ckv': nrm(ks[2], (DEPTH, n_pool, PAGE_SIZE, KV_LORA), 1.0),
        'cache_mla_kpe': nrm(ks[3], (DEPTH, n_pool, PAGE_SIZE, MLA_ROPE), 1.0),
        'cache_mla_kinv': jax.random.uniform(ks[4], (DEPTH, n_pool, PAGE_SIZE, MLA_HEADS), f32, 0.8, 1.2),
        'cache_sb_k': nrm(ks[5], (DEPTH, n_pool, PAGE_SIZE, SB_HEADS, SB_DIM), 1.0),
        'cache_sb_v': nrm(ks[6], (DEPTH, n_pool, PAGE_SIZE, SB_HEADS, SB_DIM), 1.0),
        'page_table': page_table,
        'c_prompt': nrm(ks[8], (BATCH, D_MODEL), 1.0),
        'c_sample': nrm(ks[9], (DEC_BATCH, D_MODEL), 1.0),
        'w_ada': nrm(ks[10], (DEPTH, D_MODEL, 6 * D_MODEL), 0.5 * D_MODEL ** -0.5),
        'b_ada': nrm(ks[11], (DEPTH, 6 * D_MODEL), 0.02),
        'g_attn': gain(ks[12], (DEPTH, D_MODEL)),
        'g_moe': gain(ks[13], (DEPTH, D_MODEL)),
        'w_in': nrm(ks[14], (DEPTH, D_MODEL, IN_WIDTH), D_MODEL ** -0.5),
        'g_q_lat': gain(ks[15], (DEPTH, Q_LORA)),
        'w_uq': nrm(ks[16], (DEPTH, Q_LORA, MLA_HEADS * MLA_QK), Q_LORA ** -0.5),
        'g_kv_lat': gain(ks[17], (DEPTH, KV_LORA)),
        'w_uk': nrm(ks[18], (DEPTH, KV_LORA, MLA_HEADS * MLA_NOPE), KV_LORA ** -0.5),
        'w_uv': nrm(ks[19], (DEPTH, KV_LORA, MLA_HEADS * MLA_V), KV_LORA ** -0.5),
        'g_qk_q': gain(ks[20], (DEPTH, MLA_QK)),
        'g_qk_k': gain(ks[21], (DEPTH, MLA_QK)),
        'w_o': nrm(ks[22], (DEPTH, MIX_WIDTH, D_MODEL), MIX_WIDTH ** -0.5),
        'w_router': nrm(ks[23], (DEPTH, D_MODEL, N_EXPERTS), D_MODEL ** -0.5),
        'b_router': nrm(ks[24], (DEPTH, N_EXPERTS), 0.01),
        'w_exp_gate': nrm(ks[25], (DEPTH, N_EXPERTS, D_MODEL, D_EXPERT), D_MODEL ** -0.5),
        'w_exp_up': nrm(ks[26], (DEPTH, N_EXPERTS, D_MODEL, D_EXPERT), D_MODEL ** -0.5),
        'w_exp_down': nrm(ks[27], (DEPTH, N_EXPERTS, D_EXPERT, D_MODEL), D_EXPERT ** -0.5),
        'w_sh_gate': nrm(ks[28], (DEPTH, D_MODEL, D_SHARED), D_MODEL ** -0.5),
        'w_sh_up': nrm(ks[29], (DEPTH, D_MODEL, D_SHARED), D_MODEL ** -0.5),
        'w_sh_down': nrm(ks[30], (DEPTH, D_SHARED, D_MODEL), D_SHARED ** -0.5),
    }


def reference(x_prompt, x_sample, cache_mla_ckv, cache_mla_kpe, cache_mla_kinv, cache_sb_k, cache_sb_v,
              page_table, c_prompt, c_sample, w_ada, b_ada, g_attn, g_moe, w_in, g_q_lat, w_uq, g_kv_lat,
              w_uk, w_uv, g_qk_q, g_qk_k, w_o, w_router, b_router, w_exp_gate, w_exp_up, w_exp_down,
              w_sh_gate, w_sh_up, w_sh_down):
    y_prompt = x_prompt
    y_sample = x_sample
    p_rows = [[], [], [], [], []]
    s_rows = [[], [], [], [], []]
    for l in range(DEPTH):
        lw = dict(w_ada=w_ada[l], b_ada=b_ada[l], g_attn=g_attn[l], g_moe=g_moe[l], w_in=w_in[l],
                  g_q_lat=g_q_lat[l], w_uq=w_uq[l], g_kv_lat=g_kv_lat[l], w_uk=w_uk[l], w_uv=w_uv[l],
                  g_qk_q=g_qk_q[l], g_qk_k=g_qk_k[l], w_o=w_o[l], w_router=w_router[l], b_router=b_router[l],
                  w_exp_gate=w_exp_gate[l], w_exp_up=w_exp_up[l], w_exp_down=w_exp_down[l],
                  w_sh_gate=w_sh_gate[l], w_sh_up=w_sh_up[l], w_sh_down=w_sh_down[l])
        y_prompt, new_p = layer_prompt(y_prompt, c_prompt, lw)
        y_sample, new_s = layer_sample(y_sample, c_sample, l, page_table, cache_mla_ckv, cache_mla_kpe,
                                       cache_mla_kinv, cache_sb_k, cache_sb_v, lw)
        for lst, a in zip(p_rows, new_p):
            lst.append(a)
        for lst, a in zip(s_rows, new_s):
            lst.append(a)
    p_ckv, p_kpe, p_kinv, p_sbk, p_sbv = [jnp.stack(r) for r in p_rows]
    s_ckv, s_kpe, s_kinv, s_sbk, s_sbv = [jnp.stack(r) for r in s_rows]
    return (y_prompt, y_sample, p_ckv, p_kpe, p_kinv, p_sbk, p_sbv, s_ckv, s_kpe, s_kinv, s_sbk, s_sbv)
```

```python
import functools

import jax
import jax.numpy as jnp
from jax import lax
from jax.experimental import pallas as pl
from jax.experimental.pallas import tpu as pltpu

F32 = jnp.float32
BF16 = jnp.bfloat16
I32 = jnp.int32

D_MODEL = 1024
PAGE = 128
N_HEADS = 8
NOPE = 64
ROPE = 32
QK = NOPE + ROPE
HEAD_PAD = 128
Q_LORA = 768
KV_LORA = 256
SB_DIM = 64
SB_WIDTH = N_HEADS * SB_DIM
ROPE_THETA = 10000.0
MLA_SCALE = QK ** -0.5
SB_SCALE = SB_DIM ** -0.5
N_EXPERTS = 256
TOP_K = 8
N_GROUPS = 8
GROUP_SIZE = N_EXPERTS // N_GROUPS
TOPK_GROUPS = 4
D_EXPERT = 256
D_SHARED = 256
ROUTE_SCALE = 2.5
EPS = 1e-6
LANES = 128
SUBLANES = 8
ROW_CHUNKS = D_MODEL // LANES
SB_DONE = -110.0
EXPERT_BLOCK = 256
VMEM_LIMIT = 56 * 1024 * 1024

NT = (((1,), (1,)), ((), ()))


def _params(sem):
    return pltpu.CompilerParams(dimension_semantics=sem, vmem_limit_bytes=VMEM_LIMIT)


def _dot(a, b):
    return jnp.dot(a, b, preferred_element_type=F32)


def _dot_nt(a, b):
    return lax.dot_general(a, b, NT, preferred_element_type=F32)


def _split_bf16(x):
    hi = x.astype(BF16)
    lo = (x - hi.astype(F32)).astype(BF16)
    return hi, lo


def _sigmoid(x):
    return 1.0 / (1.0 + jnp.exp(-x))


def _softplus(x):
    return jnp.maximum(x, 0.0) + jnp.log(1.0 + jnp.exp(-jnp.abs(x)))


def _rms(x, g):
    return x * lax.rsqrt(jnp.mean(x * x, axis=-1, keepdims=True) + EPS) * g


def _rope128(x, cos_t, sin_t):
    lane = lax.broadcasted_iota(I32, x.shape, 1)
    up = pltpu.roll(x, LANES - ROPE // 2, 1)
    dn = pltpu.roll(x, ROPE // 2, 1)
    rot = jnp.where(lane < NOPE + ROPE // 2, up, dn)
    return x * cos_t + rot * sin_t


def _ada_kernel(c_ref, w_ref, b_ref, o_ref):
    c = c_ref[...]
    s = c * _sigmoid(c)
    o_ref[...] = _dot(s.astype(BF16), w_ref[...].astype(BF16)) + b_ref[...]


def _adaln(c, w_ada, b_ada):
    rows = c.shape[0]
    n = w_ada.shape[1]
    return pl.pallas_call(
        _ada_kernel,
        grid=(n // D_MODEL,),
        in_specs=[pl.BlockSpec((rows, D_MODEL), lambda j: (0, 0)),
                  pl.BlockSpec((D_MODEL, D_MODEL), lambda j: (0, j)),
                  pl.BlockSpec((1, D_MODEL), lambda j: (0, j))],
        out_specs=pl.BlockSpec((rows, D_MODEL), lambda j: (0, j)),
        out_shape=jax.ShapeDtypeStruct((rows, n), F32),
        compiler_params=_params(("arbitrary",)),
        name="adaln",
    )(c, w_ada, b_ada.reshape(1, n))


def _pre_kernel(x_ref, sh_ref, sc_ref, cos_ref, sin_ref, gattn_ref, gql_ref, gkv_ref, gq_ref, gkr_ref,
                win_ref, wuq_ref, wuk_ref, wuv_ref,
                q_ref, k_ref, v_ref, sbq_ref, sbkb_ref, sbvb_ref, ckv_ref, kpe_ref, kinv_ref,
                sbk_ref, sbv_ref, *maybe_qabs_ref):
    x = x_ref[...]
    h = _rms(x, gattn_ref[...]) * (1.0 + sc_ref[0]) + sh_ref[0]
    proj = _dot(h.astype(BF16), win_ref[...])
    o_kv = Q_LORA
    o_sbq = o_kv + KV_LORA
    o_sbk = o_sbq + SB_WIDTH
    o_sbv = o_sbk + SB_WIDTH
    o_kr = o_sbv + SB_WIDTH
    q_lat = proj[:, :o_kv]
    kv_lat = proj[:, o_kv:o_sbq]
    sbq = proj[:, o_sbq:o_sbk]
    sbk = proj[:, o_sbk:o_sbv]
    sbv = proj[:, o_sbv:o_kr]
    kr = proj[:, o_kr:o_kr + HEAD_PAD]

    cos_t = cos_ref[...]
    sin_t = sin_ref[...]

    sbq_ref[...] = (sbq * SB_SCALE).astype(BF16)
    sbkb_ref[...] = sbk.astype(BF16)
    sbvb_ref[...] = sbv.astype(BF16)
    sbk_ref[...] = sbk
    sbv_ref[...] = sbv

    q = _dot(_rms(q_lat, gql_ref[...]).astype(BF16), wuq_ref[...])
    gq = gq_ref[...]
    for hd in range(N_HEADS):
        qh = q[:, hd * HEAD_PAD:(hd + 1) * HEAD_PAD]
        inv = lax.rsqrt(jnp.sum(qh * qh, axis=-1, keepdims=True) * (1.0 / QK) + EPS)
        qh = _rope128(qh * inv * gq, cos_t, sin_t)
        qhb = qh.astype(BF16)
        q_ref[:, hd * HEAD_PAD:(hd + 1) * HEAD_PAD] = qhb
        if maybe_qabs_ref:
            wk_h = wuk_ref[:, hd * HEAD_PAD:(hd + 1) * HEAD_PAD]
            maybe_qabs_ref[0][:, hd * KV_LORA:(hd + 1) * KV_LORA] = _dot_nt(qhb, wk_h).astype(BF16)

    ckv = _rms(kv_lat, gkv_ref[...])
    ckv_ref[...] = ckv
    cb = ckv.astype(BF16)
    kn = _dot(cb, wuk_ref[...])
    v_ref[...] = _dot(cb, wuv_ref[...]).astype(BF16)
    kpe = _rope128(kr * gkr_ref[...], cos_t, sin_t)
    kpe_ref[...] = kpe
    lane = lax.broadcasted_iota(I32, kpe.shape, 1)
    kinv_all = jnp.zeros_like(kpe)
    for hd in range(N_HEADS):
        knh = kn[:, hd * HEAD_PAD:(hd + 1) * HEAD_PAD]
        kp = knh + kr
        inv = lax.rsqrt(jnp.sum(kp * kp, axis=-1, keepdims=True) * (1.0 / QK) + EPS)
        k_ref[:, hd * HEAD_PAD:(hd + 1) * HEAD_PAD] = ((knh + kpe) * (inv * MLA_SCALE)).astype(BF16)
        kinv_all = jnp.where(lane == hd, inv, kinv_all)
    kinv_ref[...] = kinv_all


def _pre_project(x, shift, scale, cos_t, sin_t, wts, *, tm, per_token_mod, pos_tiles, with_qabs):
    t = x.shape[0]
    n_tiles = t // tm
    if per_token_mod:
        mod_spec = pl.BlockSpec((1, tm, D_MODEL), lambda i: (i, 0, 0))
    else:
        tiles_per_seq = n_tiles // shift.shape[0]
        mod_spec = pl.BlockSpec((1, 1, D_MODEL), lambda i: (i // tiles_per_seq, 0, 0))
    tab_spec = pl.BlockSpec((tm, HEAD_PAD), lambda i: (i % pos_tiles, 0))

    def full(a):
        return pl.BlockSpec(a.shape, lambda i: (0,) * a.ndim)

    def rows(width):
        return pl.BlockSpec((tm, width), lambda i: (i, 0))

    w_in_w = wts["w_in"].shape[1]
    hp = N_HEADS * HEAD_PAD
    out_shapes = [
        jax.ShapeDtypeStruct((t, hp), BF16), jax.ShapeDtypeStruct((t, hp), BF16),
        jax.ShapeDtypeStruct((t, hp), BF16),
        jax.ShapeDtypeStruct((t, SB_WIDTH), BF16), jax.ShapeDtypeStruct((t, SB_WIDTH), BF16),
        jax.ShapeDtypeStruct((t, SB_WIDTH), BF16),
        jax.ShapeDtypeStruct((t, KV_LORA), F32), jax.ShapeDtypeStruct((t, HEAD_PAD), F32),
        jax.ShapeDtypeStruct((t, HEAD_PAD), F32),
        jax.ShapeDtypeStruct((t, SB_WIDTH), F32), jax.ShapeDtypeStruct((t, SB_WIDTH), F32),
    ]
    out_specs = [rows(hp), rows(hp), rows(hp), rows(SB_WIDTH), rows(SB_WIDTH), rows(SB_WIDTH),
                 rows(KV_LORA), rows(HEAD_PAD), rows(HEAD_PAD), rows(SB_WIDTH), rows(SB_WIDTH)]
    if with_qabs:
        out_shapes.append(jax.ShapeDtypeStruct((t, N_HEADS * KV_LORA), BF16))
        out_specs.append(rows(N_HEADS * KV_LORA))
    del w_in_w
    return pl.pallas_call(
        _pre_kernel,
        grid=(n_tiles,),
        in_specs=[rows(D_MODEL), mod_spec, mod_spec, tab_spec, tab_spec,
                  full(wts["g_attn"]), full(wts["g_q_lat"]), full(wts["g_kv_lat"]), full(wts["gq"]),
                  full(wts["gkr"]), full(wts["w_in"]), full(wts["w_uq"]), full(wts["w_uk"]),
                  full(wts["w_uv"])],
        out_specs=out_specs,
        out_shape=out_shapes,
        compiler_params=_params(("arbitrary",)),
        name="pre_project",
    )(x, shift, scale, cos_t, sin_t, wts["g_attn"], wts["g_q_lat"], wts["g_kv_lat"], wts["gq"],
      wts["gkr"], wts["w_in"], wts["w_uq"], wts["w_uk"], wts["w_uv"])


def _mla_prompt_kernel(q_ref, k_ref, v_ref, o_ref, *, blk):
    qi = pl.program_id(2)
    q = q_ref[...]

    def step(j, carry, diagonal):
        m, l, acc = carry
        start = pl.multiple_of(j * blk, blk)
        k = k_ref[pl.ds(start, blk), :]
        v = v_ref[pl.ds(start, blk), :]
        s = _dot_nt(q, k)
        if diagonal:
            row = lax.broadcasted_iota(I32, s.shape, 0)
            col = lax.broadcasted_iota(I32, s.shape, 1)
            s = jnp.where(col <= row, s, -jnp.inf)
        m_new = jnp.maximum(m, jnp.max(s, axis=1, keepdims=True))
        alpha = jnp.exp(m - m_new)
        p = jnp.exp(s - m_new)
        l = alpha * l + jnp.sum(p, axis=1, keepdims=True)
        acc = alpha * acc + _dot(p.astype(BF16), v)
        return m_new, l, acc

    init = (jnp.full((blk, 1), -jnp.inf, F32), jnp.zeros((blk, 1), F32), jnp.zeros((blk, HEAD_PAD), F32))
    carry = lax.fori_loop(0, qi, lambda j, c: step(j, c, False), init)
    _, l, acc = step(qi, carry, True)
    o_ref[...] = (acc / l).astype(BF16)


def _mla_prompt(q, k, v, batch, seq, blk):
    nq = seq // blk
    return pl.pallas_call(
        functools.partial(_mla_prompt_kernel, blk=blk),
        grid=(batch, N_HEADS, nq),
        in_specs=[pl.BlockSpec((blk, HEAD_PAD), lambda b, h, i: (b * nq + i, h)),
                  pl.BlockSpec((seq, HEAD_PAD), lambda b, h, i: (b, h)),
                  pl.BlockSpec((seq, HEAD_PAD), lambda b, h, i: (b, h))],
        out_specs=pl.BlockSpec((blk, HEAD_PAD), lambda b, h, i: (b * nq + i, h)),
        out_shape=jax.ShapeDtypeStruct(q.shape, BF16),
        compiler_params=_params(("arbitrary", "arbitrary", "arbitrary")),
        name="mla_prompt",
    )(q, k, v)


def _sb_prompt_kernel(q_ref, k_ref, v_ref, o_ref, *, blk):
    qi = pl.program_id(2)
    q2 = q_ref[...]
    lane = lax.broadcasted_iota(I32, q2.shape, 1)
    row = lax.broadcasted_iota(I32, (blk, blk), 0)
    col = lax.broadcasted_iota(I32, (blk, blk), 1)
    later_mat = (row > col).astype(BF16)
    heads = []
    for hh in range(2):
        qm = jnp.where((lane >= SB_DIM) == (hh == 1), q2, jnp.zeros_like(q2))

        def cond(c):
            return jnp.logical_and(c[0] >= 0, c[1] == 0)

        def body(c):
            j, _, tail, acc = c
            start = pl.multiple_of(j * blk, blk)
            k = k_ref[pl.ds(start, blk), :]
            v = v_ref[pl.ds(start, blk), :]
            z = _dot_nt(qm, k)
            mask = (col + j * blk) < (row + qi * blk)
            sp = _softplus(z)
            lf = jnp.where(mask, -sp, 0.0)
            hi, lo = _split_bf16(lf)
            later = _dot(hi, later_mat) + _dot(lo, later_mat)
            a = jnp.where(mask, jnp.exp((z - sp) + later + tail), 0.0)
            acc = acc + _dot(a.astype(BF16), v)
            tail = tail + jnp.sum(lf, axis=1, keepdims=True)
            done = (jnp.max(tail) < SB_DONE).astype(I32)
            return j - 1, done, tail, acc

        init = (qi, jnp.int32(0), jnp.zeros((blk, 1), F32), jnp.zeros((blk, 2 * SB_DIM), F32))
        heads.append(lax.while_loop(cond, body, init)[3])
    o_ref[...] = jnp.where(lane < SB_DIM, heads[0], heads[1]).astype(BF16)


def _sb_prompt(q, k, v, batch, seq, blk):
    nq = seq // blk
    pairs = SB_WIDTH // LANES
    return pl.pallas_call(
        functools.partial(_sb_prompt_kernel, blk=blk),
        grid=(batch, pairs, nq),
        in_specs=[pl.BlockSpec((blk, LANES), lambda b, h, i: (b * nq + i, h)),
                  pl.BlockSpec((seq, LANES), lambda b, h, i: (b, h)),
                  pl.BlockSpec((seq, LANES), lambda b, h, i: (b, h))],
        out_specs=pl.BlockSpec((blk, LANES), lambda b, h, i: (b * nq + i, h)),
        out_shape=jax.ShapeDtypeStruct(q.shape, BF16),
        compiler_params=_params(("arbitrary", "arbitrary", "arbitrary")),
        name="sb_prompt",
    )(q, k, v)


def _mla_decode_kernel(pt_ref, qabs_ref, qrope_ref, cnew_ref, penew_ref, invnew_ref, *refs, n_pg):
    del pt_ref
    ckv_refs = refs[:n_pg]
    kpe_refs = refs[n_pg:2 * n_pg]
    kinv_refs = refs[2 * n_pg:3 * n_pg]
    o_ref = refs[3 * n_pg]
    m_ref, l_ref, acc_ref = refs[3 * n_pg + 1:]
    j = pl.program_id(1)
    qa = qabs_ref[0]
    qr = qrope_ref[0]

    @pl.when(j == 0)
    def _():
        c_new = cnew_ref[0]
        s0 = (jnp.sum(qa.astype(F32) * c_new, axis=1, keepdims=True)
              + jnp.sum(qr.astype(F32) * penew_ref[0], axis=1, keepdims=True))
        rows = lax.broadcasted_iota(I32, (LANES, LANES), 0)
        cols = lax.broadcasted_iota(I32, (LANES, LANES), 1)
        s0_l = jnp.sum(jnp.where(rows == cols, s0, 0.0), axis=0, keepdims=True)
        m_ref[...] = s0_l * invnew_ref[0] * MLA_SCALE
        l_ref[...] = jnp.ones_like(l_ref)
        acc_ref[...] = jnp.broadcast_to(c_new, acc_ref.shape)

    ckv = jnp.concatenate([r[0, 0] for r in ckv_refs], axis=0)
    kpe = jnp.concatenate([r[0, 0] for r in kpe_refs], axis=0)
    kinv = jnp.concatenate([r[0, 0] for r in kinv_refs], axis=0)
    cb = ckv.astype(BF16)
    s = _dot_nt(cb, qa) + _dot_nt(kpe.astype(BF16), qr)
    lane = lax.broadcasted_iota(I32, s.shape, 1)
    kinv_l = jnp.concatenate([kinv, jnp.ones((kinv.shape[0], LANES - N_HEADS), F32)], axis=1)
    s = s * kinv_l * MLA_SCALE
    m_old = m_ref[...]
    m_new = jnp.maximum(m_old, jnp.max(s, axis=0, keepdims=True))
    alpha = jnp.exp(m_old - m_new)
    p = jnp.where(lane < N_HEADS, jnp.exp(s - m_new), 0.0)
    l_ref[...] = alpha * l_ref[...] + jnp.sum(p, axis=0, keepdims=True)
    m_ref[...] = m_new
    pt = jnp.transpose(p)
    rows = lax.broadcasted_iota(I32, (LANES, LANES), 0)
    cols = lax.broadcasted_iota(I32, (LANES, LANES), 1)
    alpha_r = jnp.sum(jnp.where(rows == cols, alpha, 0.0), axis=1, keepdims=True)
    acc_ref[...] = alpha_r * acc_ref[...] + _dot(pt.astype(BF16), cb)

    @pl.when(j == pl.num_programs(1) - 1)
    def _():
        l_r = jnp.sum(jnp.where(rows == cols, l_ref[...], 0.0), axis=1, keepdims=True)
        o_ref[0] = (acc_ref[...] / jnp.where(l_r > 0.0, l_r, 1.0))[:N_HEADS].astype(o_ref.dtype)


def _mla_decode(layer, page_table, qabs, qrope, c_new, pe_new, inv_new, cache_ckv, cache_kpe, cache_kinv, n_pg):
    nb, n_pages = page_table.shape
    steps = n_pages // n_pg

    def pg(width, p):
        return pl.BlockSpec((1, 1, PAGE, width), lambda b, j, pt: (layer, pt[b, j * n_pg + p], 0, 0))

    in_specs = [pl.BlockSpec((1, LANES, KV_LORA), lambda b, j, pt: (b, 0, 0)),
                pl.BlockSpec((1, LANES, ROPE), lambda b, j, pt: (b, 0, 0)),
                pl.BlockSpec((1, 1, KV_LORA), lambda b, j, pt: (b, 0, 0)),
                pl.BlockSpec((1, 1, ROPE), lambda b, j, pt: (b, 0, 0)),
                pl.BlockSpec((1, 1, LANES), lambda b, j, pt: (b, 0, 0))]
    in_specs += [pg(KV_LORA, p) for p in range(n_pg)]
    in_specs += [pg(ROPE, p) for p in range(n_pg)]
    in_specs += [pg(N_HEADS, p) for p in range(n_pg)]
    return pl.pallas_call(
        functools.partial(_mla_decode_kernel, n_pg=n_pg),
        grid_spec=pltpu.PrefetchScalarGridSpec(
            num_scalar_prefetch=1,
            grid=(nb, steps),
            in_specs=in_specs,
            out_specs=pl.BlockSpec((1, N_HEADS, KV_LORA), lambda b, j, pt: (b, 0, 0)),
            scratch_shapes=[pltpu.VMEM((1, LANES), F32), pltpu.VMEM((1, LANES), F32),
                            pltpu.VMEM((LANES, KV_LORA), F32)]),
        out_shape=jax.ShapeDtypeStruct((nb, N_HEADS, KV_LORA), BF16),
        compiler_params=_params(("arbitrary", "arbitrary")),
        name="mla_decode",
    )(page_table, qabs, qrope, c_new, pe_new, inv_new,
      *([cache_ckv] * n_pg), *([cache_kpe] * n_pg), *([cache_kinv] * n_pg))


def _sb_page(q, k3, v3, tail, acc):
    z = jnp.sum(k3 * q[None], axis=-1, keepdims=True)
    sp = _softplus(z)
    lf = -sp
    suf = lf
    shift = 1
    n = lf.shape[0]
    while shift < n:
        suf = suf + jnp.concatenate([suf[shift:], jnp.zeros((shift,) + suf.shape[1:], F32)], axis=0)
        shift *= 2
    later = suf - lf
    a = jnp.exp((z - sp) + later + tail[None])
    acc = acc + jnp.sum(a * v3, axis=0)
    tail = tail + suf[0]
    return tail, acc


def _sb_decode_kernel(pt_ref, q_ref, k0_ref, v0_ref, k1_ref, v1_ref, kc_ref, vc_ref, o_ref,
                      kbuf, vbuf, sem, *, layer, n_pages, n_pre):
    b = pl.program_id(0)
    q = q_ref[0]
    tail = jnp.zeros((N_HEADS, 1), F32)
    acc = jnp.zeros((N_HEADS, SB_DIM), F32)
    tail, acc = _sb_page(q, k0_ref[0, 0], v0_ref[0, 0], tail, acc)
    if n_pre > 1:
        tail, acc = _sb_page(q, k1_ref[0, 0], v1_ref[0, 0], tail, acc)

    def cond(c):
        return jnp.logical_and(c[0] >= 0, c[1] == 0)

    def body(c):
        j, _, tail, acc = c
        page = pt_ref[b, j]
        ck = pltpu.make_async_copy(kc_ref.at[layer, page], kbuf, sem.at[0])
        cv = pltpu.make_async_copy(vc_ref.at[layer, page], vbuf, sem.at[1])
        ck.start()
        cv.start()
        ck.wait()
        cv.wait()
        tail, acc = _sb_page(q, kbuf[...], vbuf[...], tail, acc)
        return j - 1, (jnp.max(tail) < SB_DONE).astype(I32), tail, acc

    done0 = (jnp.max(tail) < SB_DONE).astype(I32)
    _, _, _, acc = lax.while_loop(cond, body, (jnp.int32(n_pages - 1 - n_pre), done0, tail, acc))
    o_ref[0] = acc


def _sb_decode(layer, page_table, q, cache_k, cache_v):
    nb, n_pages = page_table.shape
    n_pre = 2 if n_pages >= 2 else 1

    def pg(back):
        return pl.BlockSpec((1, 1, PAGE, N_HEADS, SB_DIM),
                            lambda b, pt: (layer, pt[b, n_pages - 1 - back], 0, 0, 0))

    return pl.pallas_call(
        functools.partial(_sb_decode_kernel, layer=layer, n_pages=n_pages, n_pre=n_pre),
        grid_spec=pltpu.PrefetchScalarGridSpec(
            num_scalar_prefetch=1,
            grid=(nb,),
            in_specs=[pl.BlockSpec((1, N_HEADS, SB_DIM), lambda b, pt: (b, 0, 0)),
                      pg(0), pg(0), pg(n_pre - 1), pg(n_pre - 1),
                      pl.BlockSpec(memory_space=pl.ANY), pl.BlockSpec(memory_space=pl.ANY)],
            out_specs=pl.BlockSpec((1, N_HEADS, SB_DIM), lambda b, pt: (b, 0, 0)),
            scratch_shapes=[pltpu.VMEM((PAGE, N_HEADS, SB_DIM), F32),
                            pltpu.VMEM((PAGE, N_HEADS, SB_DIM), F32),
                            pltpu.SemaphoreType.DMA((2,))]),
        out_shape=jax.ShapeDtypeStruct((nb, N_HEADS, SB_DIM), F32),
        compiler_params=_params(("arbitrary",)),
        name="sb_decode",
    )(page_table, q, cache_k, cache_v, cache_k, cache_v, cache_k, cache_v)


def _uv_kernel(o_ref, w_ref, out_ref):
    for hd in range(N_HEADS):
        out_ref[:, hd * HEAD_PAD:(hd + 1) * HEAD_PAD] = _dot(
            o_ref[:, hd * KV_LORA:(hd + 1) * KV_LORA], w_ref[:, hd * HEAD_PAD:(hd + 1) * HEAD_PAD]
        ).astype(BF16)


def _value_up(o_lat, w_uv_p):
    t = o_lat.shape[0]
    return pl.pallas_call(
        _uv_kernel,
        out_shape=jax.ShapeDtypeStruct((t, N_HEADS * HEAD_PAD), BF16),
        compiler_params=pltpu.CompilerParams(vmem_limit_bytes=VMEM_LIMIT),
        name="value_up",
    )(o_lat, w_uv_p)


def _post_kernel(x_ref, om_ref, os_ref, ga_ref, shm_ref, scm_ref, gm_ref, gmoe_ref, wom_ref, wos_ref,
                 wgu_ref, wd_ref, base_ref, h2_ref):
    attn = _dot(om_ref[...], wom_ref[...]) + _dot(os_ref[...], wos_ref[...])
    xm = x_ref[...] + ga_ref[0] * attn
    h2 = _rms(xm, gmoe_ref[...]) * (1.0 + scm_ref[0]) + shm_ref[0]
    h2_ref[...] = h2
    gu = _dot(h2.astype(BF16), wgu_ref[...])
    g = gu[:, :D_SHARED]
    act = (g * _sigmoid(g)) * gu[:, D_SHARED:]
    ysh = _dot(act.astype(BF16), wd_ref[...])
    base_ref[...] = xm + gm_ref[0] * ysh


def _post_attention(x, o_mla, o_sb, mods, wts, *, tm, per_token_mod):
    t = x.shape[0]
    n_tiles = t // tm
    g_a, sh_m, sc_m, g_m = mods
    if per_token_mod:
        mod_spec = pl.BlockSpec((1, tm, D_MODEL), lambda i: (i, 0, 0))
    else:
        tiles_per_seq = n_tiles // g_a.shape[0]
        mod_spec = pl.BlockSpec((1, 1, D_MODEL), lambda i: (i // tiles_per_seq, 0, 0))

    def full(a):
        return pl.BlockSpec(a.shape, lambda i: (0,) * a.ndim)

    def rows(width):
        return pl.BlockSpec((tm, width), lambda i: (i, 0))

    buf = jax.ShapeDtypeStruct((t, D_MODEL), F32)
    return pl.pallas_call(
        _post_kernel,
        grid=(n_tiles,),
        in_specs=[rows(D_MODEL), rows(N_HEADS * HEAD_PAD), rows(SB_WIDTH), mod_spec, mod_spec, mod_spec,
                  mod_spec, full(wts["g_moe"]), full(wts["w_o_mla"]), full(wts["w_o_sb"]),
                  full(wts["w_sh_gu"]), full(wts["w_sh_down"])],
        out_specs=[rows(D_MODEL), rows(D_MODEL)],
        out_shape=[buf, buf],
        compiler_params=_params(("arbitrary",)),
        name="post_attention",
    )(x, o_mla, o_sb, g_a, sh_m, sc_m, g_m, wts["g_moe"], wts["w_o_mla"], wts["w_o_sb"],
      wts["w_sh_gu"], wts["w_sh_down"])


def _router_kernel(hp_ref, hs_ref, whi_ref, wlo_ref, b_ref, idx_ref, wgt_ref, rank_ref, cnt_ref, *,
                   n_prompt_tiles):
    i = pl.program_id(0)

    @pl.when(i == 0)
    def _():
        cnt_ref[...] = jnp.zeros_like(cnt_ref)

    h = jnp.where(i < n_prompt_tiles, hp_ref[...], hs_ref[...])
    tm = h.shape[0]
    hhi, hlo = _split_bf16(h)
    whi = whi_ref[...]
    logits = _dot_nt(whi, hhi) + _dot_nt(whi, hlo) + _dot_nt(wlo_ref[...], hhi)
    scores = _sigmoid(logits)
    sel = scores + b_ref[...]
    neg = -jnp.inf

    blocks = [sel[g * GROUP_SIZE:(g + 1) * GROUP_SIZE, :] for g in range(N_GROUPS)]
    gs = []
    for blk in blocks:
        m1 = jnp.max(blk, axis=0, keepdims=True)
        is_max = blk == m1
        n_max = jnp.sum(is_max.astype(F32), axis=0, keepdims=True)
        m2 = jnp.max(jnp.where(is_max, neg, blk), axis=0, keepdims=True)
        gs.append(m1 + jnp.where(n_max >= 2.0, m1, m2))
    kept = []
    for g in range(N_GROUPS):
        ahead = jnp.zeros_like(gs[g])
        for g2 in range(N_GROUPS):
            if g2 == g:
                continue
            better = gs[g2] > gs[g]
            if g2 < g:
                better = jnp.logical_or(better, gs[g2] == gs[g])
            ahead = ahead + better.astype(F32)
        kept.append(jnp.where(ahead < float(TOPK_GROUPS), blocks[g], neg))
    cur = jnp.concatenate(kept, axis=0)

    rowf = lax.broadcasted_iota(I32, cur.shape, 0).astype(F32)
    chosen = jnp.zeros_like(cur)
    idxs, ws = [], []
    for _ in range(TOP_K):
        m = jnp.max(cur, axis=0, keepdims=True)
        ik = jnp.min(jnp.where(cur == m, rowf, float(N_EXPERTS)), axis=0, keepdims=True)
        hit = rowf == ik
        ws.append(jnp.sum(jnp.where(hit, scores, 0.0), axis=0, keepdims=True))
        cur = jnp.where(hit, neg, cur)
        chosen = chosen + hit.astype(F32)
        idxs.append(ik)
    wsum = ws[0]
    for w in ws[1:]:
        wsum = wsum + w
    ws = [w / wsum * ROUTE_SCALE for w in ws]

    r2 = lax.broadcasted_iota(I32, (tm, tm), 0)
    c2 = lax.broadcasted_iota(I32, (tm, tm), 1)
    left = (r2 < c2).astype(BF16)
    pos = _dot(chosen.astype(BF16), left) + cnt_ref[...]
    ranks = [jnp.sum(jnp.where(rowf == ik, pos, 0.0), axis=0, keepdims=True) for ik in idxs]
    cnt_ref[...] = cnt_ref[...] + jnp.sum(chosen, axis=1, keepdims=True)

    idx_ref[...] = jnp.concatenate(idxs, axis=0).astype(I32)
    wgt_ref[...] = jnp.concatenate(ws, axis=0)
    rank_ref[...] = jnp.concatenate(ranks, axis=0).astype(I32)


def _route(h2_p, h2_s, whi, wlo, b_col, tm):
    n_p = h2_p.shape[0] // tm
    t = h2_p.shape[0] + h2_s.shape[0]
    sm = pl.BlockSpec((TOP_K, tm), lambda i: (0, i))
    return pl.pallas_call(
        functools.partial(_router_kernel, n_prompt_tiles=n_p),
        grid=(t // tm,),
        in_specs=[pl.BlockSpec((tm, D_MODEL), lambda i: (jnp.minimum(i, n_p - 1), 0)),
                  pl.BlockSpec((tm, D_MODEL), lambda i: (jnp.maximum(i - n_p, 0), 0)),
                  pl.BlockSpec(whi.shape, lambda i: (0, 0)),
                  pl.BlockSpec(wlo.shape, lambda i: (0, 0)),
                  pl.BlockSpec(b_col.shape, lambda i: (0, 0))],
        out_specs=[sm, sm, sm, pl.BlockSpec((N_EXPERTS, 1), lambda i: (0, 0))],
        out_shape=[jax.ShapeDtypeStruct((TOP_K, t), I32), jax.ShapeDtypeStruct((TOP_K, t), F32),
                   jax.ShapeDtypeStruct((TOP_K, t), I32), jax.ShapeDtypeStruct((N_EXPERTS, 1), F32)],
        compiler_params=_params(("arbitrary",)),
        name="router",
    )(h2_p, h2_s, whi, wlo, b_col)


def _dispatch_kernel(pstart_ref, idx_ref, rank_ref, hp3_ref, hs3_ref, xs_ref, sem, *, tm, n_prompt_tiles):
    i = pl.program_id(0)

    def scatter_rows(src_ref, base):
        def issue(t, carry):
            for k in range(TOP_K):
                dst = pstart_ref[idx_ref[k, t]] + rank_ref[k, t]
                pltpu.make_async_copy(src_ref.at[base + t], xs_ref.at[dst], sem).start()
            return carry

        lax.fori_loop(0, tm, issue, 0)

        def drain(t, carry):
            for _ in range(TOP_K):
                pltpu.make_async_copy(src_ref.at[0], xs_ref.at[0], sem).wait()
            return carry

        lax.fori_loop(0, tm, drain, 0)

    @pl.when(i < n_prompt_tiles)
    def _():
        scatter_rows(hp3_ref, i * tm)

    @pl.when(i >= n_prompt_tiles)
    def _():
        scatter_rows(hs3_ref, (i - n_prompt_tiles) * tm)


def _dispatch(pstart, idx, rank, hp3, hs3, n_rows, tm):
    n_p = hp3.shape[0] // tm
    t = hp3.shape[0] + hs3.shape[0]
    sm = pl.BlockSpec((TOP_K, tm), lambda i, ps: (0, i), memory_space=pltpu.SMEM)
    return pl.pallas_call(
        functools.partial(_dispatch_kernel, tm=tm, n_prompt_tiles=n_p),
        grid_spec=pltpu.PrefetchScalarGridSpec(
            num_scalar_prefetch=1,
            grid=(t // tm,),
            in_specs=[sm, sm, pl.BlockSpec(memory_space=pl.ANY), pl.BlockSpec(memory_space=pl.ANY)],
            out_specs=pl.BlockSpec(memory_space=pl.ANY),
            scratch_shapes=[pltpu.SemaphoreType.DMA(())]),
        out_shape=jax.ShapeDtypeStruct((n_rows, ROW_CHUNKS, LANES), F32),
        compiler_params=_params(("arbitrary",)),
        name="dispatch",
    )(pstart, idx, rank, hp3, hs3)


def _expert_kernel(be_ref, nb_ref, xs_ref, wg_ref, wu_ref, wd_ref, ys_ref, wgu_s, wd_s):
    i = pl.program_id(0)

    @pl.when(i < nb_ref[0])
    def _():
        prev = be_ref[jnp.maximum(i - 1, 0)]

        @pl.when(jnp.logical_or(i == 0, be_ref[i] != prev))
        def _():
            wgu_s[:, :D_EXPERT] = wg_ref[0].astype(BF16)
            wgu_s[:, D_EXPERT:] = wu_ref[0].astype(BF16)
            wd_s[...] = wd_ref[0].astype(BF16)

        x = jnp.concatenate([xs_ref[:, c, :] for c in range(ROW_CHUNKS)], axis=1).astype(BF16)
        gu = _dot(x, wgu_s[...])
        g = gu[:, :D_EXPERT]
        act = (g * _sigmoid(g)) * gu[:, D_EXPERT:]
        y = _dot(act.astype(BF16), wd_s[...])
        for c in range(ROW_CHUNKS):
            ys_ref[:, c, :] = y[:, c * LANES:(c + 1) * LANES]


def _experts(layer, block_e, n_used, xs, w_gate, w_up, w_down):
    n_rows = xs.shape[0]
    n_blocks = n_rows // EXPERT_BLOCK
    w_gate, w_up, w_down = (w.reshape((-1,) + w.shape[2:]) for w in (w_gate, w_up, w_down))

    def row_map(i, be, nb):
        return (jnp.minimum(i, nb[0] - 1), 0, 0)

    def w_map(i, be, nb):
        return (layer * N_EXPERTS + be[jnp.minimum(i, nb[0] - 1)], 0, 0)

    return pl.pallas_call(
        _expert_kernel,
        grid_spec=pltpu.PrefetchScalarGridSpec(
            num_scalar_prefetch=2,
            grid=(n_blocks,),
            in_specs=[pl.BlockSpec((EXPERT_BLOCK, ROW_CHUNKS, LANES), row_map),
                      pl.BlockSpec((1, D_MODEL, D_EXPERT), w_map),
                      pl.BlockSpec((1, D_MODEL, D_EXPERT), w_map),
                      pl.BlockSpec((1, D_EXPERT, D_MODEL), w_map)],
            out_specs=pl.BlockSpec((EXPERT_BLOCK, ROW_CHUNKS, LANES), row_map),
            scratch_shapes=[pltpu.VMEM((D_MODEL, 2 * D_EXPERT), BF16), pltpu.VMEM((D_EXPERT, D_MODEL), BF16)]),
        out_shape=jax.ShapeDtypeStruct(xs.shape, F32),
        compiler_params=_params(("arbitrary",)),
        name="experts",
    )(block_e, n_used, xs, w_gate, w_up, w_down)


def _combine_kernel(pstart_ref, idx_ref, rank_ref, wgt_ref, ys_ref, basep_ref, bases_ref, gmp_ref, gms_ref,
                    out_ref, buf, sem, *, tm, n_prompt_tiles):
    i = pl.program_id(0)

    def issue(t, carry):
        for k in range(TOP_K):
            src = pstart_ref[idx_ref[k, t]] + rank_ref[k, t]
            pltpu.make_async_copy(ys_ref.at[src], buf.at[k, t], sem).start()
        return carry

    lax.fori_loop(0, tm, issue, 0)

    def drain(t, carry):
        for k in range(TOP_K):
            pltpu.make_async_copy(ys_ref.at[0], buf.at[k, 0], sem).wait()
        return carry

    lax.fori_loop(0, tm, drain, 0)

    def mix(t):
        acc = wgt_ref[0, t] * buf[0, t]
        for k in range(1, TOP_K):
            acc = acc + wgt_ref[k, t] * buf[k, t]
        return acc

    @pl.when(i < n_prompt_tiles)
    def _():
        gm = gmp_ref[0]

        def tok(t, carry):
            out_ref[t] = basep_ref[t] + gm * mix(t)
            return carry

        lax.fori_loop(0, tm, tok, 0)

    @pl.when(i >= n_prompt_tiles)
    def _():
        def tok(t, carry):
            out_ref[t] = bases_ref[t] + gms_ref[t] * mix(t)
            return carry

        lax.fori_loop(0, tm, tok, 0)


def _combine(pstart, idx, rank, wgt, ys, base_p3, base_s3, gm_prompt3, gm_sample3, tm, seq):
    t_all = base_p3.shape[0] + base_s3.shape[0]
    n_batch = gm_prompt3.shape[0]
    n_prompt_tiles = n_batch * seq // tm
    tiles_per_seq = seq // tm
    sm = pl.BlockSpec((TOP_K, tm), lambda i, ps: (0, i), memory_space=pltpu.SMEM)
    tile = (tm, ROW_CHUNKS, LANES)
    prompt_tile = pl.BlockSpec(tile, lambda i, ps: (jnp.minimum(i, n_prompt_tiles - 1), 0, 0))
    sample_tile = pl.BlockSpec(tile, lambda i, ps: (jnp.maximum(i - n_prompt_tiles, 0), 0, 0))
    return pl.pallas_call(
        functools.partial(_combine_kernel, tm=tm, n_prompt_tiles=n_prompt_tiles),
        grid_spec=pltpu.PrefetchScalarGridSpec(
            num_scalar_prefetch=1,
            grid=(t_all // tm,),
            in_specs=[sm, sm, sm, pl.BlockSpec(memory_space=pl.ANY), prompt_tile, sample_tile,
                      pl.BlockSpec((1, ROW_CHUNKS, LANES),
                                   lambda i, ps: (jnp.minimum(i // tiles_per_seq, n_batch - 1), 0, 0)),
                      sample_tile],
            out_specs=pl.BlockSpec(tile, lambda i, ps: (i, 0, 0)),
            scratch_shapes=[pltpu.VMEM((TOP_K, tm, ROW_CHUNKS, LANES), F32), pltpu.SemaphoreType.DMA(())]),
        out_shape=jax.ShapeDtypeStruct((t_all, ROW_CHUNKS, LANES), F32),
        compiler_params=_params(("arbitrary",)),
        name="combine",
    )(pstart, idx, rank, wgt, ys, base_p3, base_s3, gm_prompt3, gm_sample3)


def _pad_heads(w, width):
    k = w.shape[0]
    w = w.reshape(k, N_HEADS, width)
    return jnp.pad(w, ((0, 0), (0, 0), (0, HEAD_PAD - width))).reshape(k, N_HEADS * HEAD_PAD)


def _rope_tables(pos):
    half = ROPE // 2
    inv_freq = ROPE_THETA ** (-jnp.arange(half, dtype=F32) * (2.0 / ROPE))
    ang = pos.astype(F32)[:, None] * inv_freq[None, :]
    cos, sin = jnp.cos(ang), jnp.sin(ang)
    n = pos.shape[0]
    ones = jnp.ones((n, NOPE), F32)
    zeros = jnp.zeros((n, HEAD_PAD - QK), F32)
    cos_t = jnp.concatenate([ones, cos, cos, zeros], axis=1)
    sin_t = jnp.concatenate([jnp.zeros((n, NOPE), F32), -sin, sin, zeros], axis=1)
    return cos_t, sin_t


def _layer_weights(l, w_ada, b_ada, g_attn, g_moe, w_in, g_q_lat, w_uq, g_kv_lat, w_uk, w_uv, g_qk_q, g_qk_k,
                   w_o, w_router, b_router, w_sh_gate, w_sh_up, w_sh_down):
    wi = w_in[l]
    o1 = Q_LORA + KV_LORA
    o2 = o1 + ROPE
    kr_block = jnp.zeros((D_MODEL, HEAD_PAD), F32).at[:, NOPE:QK].set(wi[:, o1:o2])
    w_in_p = jnp.concatenate([wi[:, :o1], wi[:, o2:], kr_block], axis=1).astype(BF16)
    gq = jnp.concatenate([g_qk_q[l][:NOPE] * g_qk_k[l][:NOPE], g_qk_q[l][NOPE:],
                          jnp.zeros((HEAD_PAD - QK,), F32)]).reshape(1, HEAD_PAD)
    gkr = jnp.concatenate([jnp.zeros((NOPE,), F32), g_qk_k[l][NOPE:],
                           jnp.zeros((HEAD_PAD - QK,), F32)]).reshape(1, HEAD_PAD)
    wo = w_o[l]
    n_mla = N_HEADS * NOPE
    w_o_mla = jnp.pad(wo[:n_mla].reshape(N_HEADS, NOPE, D_MODEL),
                      ((0, 0), (0, HEAD_PAD - NOPE), (0, 0))).reshape(N_HEADS * HEAD_PAD, D_MODEL)
    wr_t = w_router[l].T
    whi = wr_t.astype(BF16)
    wlo = (wr_t - whi.astype(F32)).astype(BF16)
    return dict(
        w_ada=w_ada[l], b_ada=b_ada[l],
        g_attn=g_attn[l].reshape(1, -1), g_moe=g_moe[l].reshape(1, -1),
        g_q_lat=g_q_lat[l].reshape(1, -1), g_kv_lat=g_kv_lat[l].reshape(1, -1), gq=gq, gkr=gkr,
        w_in=w_in_p, w_uq=_pad_heads(w_uq[l], QK).astype(BF16),
        w_uk=_pad_heads(w_uk[l], NOPE).astype(BF16), w_uv=_pad_heads(w_uv[l], NOPE).astype(BF16),
        w_o_mla=w_o_mla.astype(BF16), w_o_sb=wo[n_mla:].astype(BF16),
        w_r_hi=whi, w_r_lo=wlo, b_router=b_router[l].reshape(-1, 1),
        w_sh_gu=jnp.concatenate([w_sh_gate[l], w_sh_up[l]], axis=1).astype(BF16),
        w_sh_down=w_sh_down[l].astype(BF16),
    )


def _tile(n, pref):
    t = min(n, pref)
    while n % t:
        t //= 2
    return t


def _layer(l, xp, xs_, cache_mla_ckv, cache_mla_kpe, cache_mla_kinv, cache_sb_k, cache_sb_v, page_table,
           c_prompt, c_sample, wts, w_exp_gate, w_exp_up, w_exp_down):
    batch, seq, _ = xp.shape
    nb = xs_.shape[0]
    n_pages = page_table.shape[1]
    past = n_pages * PAGE
    t_p = batch * seq
    tok_tile = LANES
    assert xs_.shape[1] == 1 and nb % tok_tile == 0 and seq % tok_tile == 0
    t_s = nb
    tm_s = tok_tile
    t_all = t_p + t_s

    mods = _adaln(jnp.concatenate([c_prompt, c_sample], axis=0), wts["w_ada"], wts["b_ada"])
    mods_p = [m.reshape(batch, 1, D_MODEL) for m in jnp.split(mods[:batch], 6, axis=-1)]
    mods_s = [m.reshape(t_s // tm_s, tm_s, D_MODEL) for m in jnp.split(mods[batch:], 6, axis=-1)]

    tm_p = _tile(seq, 512)
    cos_p, sin_p = _rope_tables(jnp.arange(seq))
    (q, k, v, sbq, sbkb, sbvb, ckv_p, kpe_p, kinv_p, sbk_p, sbv_p) = _pre_project(
        xp.reshape(t_p, D_MODEL), mods_p[0], mods_p[1], cos_p, sin_p, wts,
        tm=tm_p, per_token_mod=False, pos_tiles=seq // tm_p, with_qabs=False)
    blk = _tile(seq, 256)
    o_mla_p = _mla_prompt(q, k, v, batch, seq, blk)
    o_sb_p = _sb_prompt(sbq, sbkb, sbvb, batch, seq, blk)

    cos_s, sin_s = _rope_tables(jnp.full((tm_s,), past, I32))
    (q_s, _, _, sbq_s, _, _, ckv_s, kpe_s, kinv_s, sbk_s, sbv_s, qabs_s) = _pre_project(
        xs_.reshape(t_s, D_MODEL), mods_s[0], mods_s[1], cos_s, sin_s, wts,
        tm=tm_s, per_token_mod=True, pos_tiles=1, with_qabs=True)
    head_rows = LANES - N_HEADS
    qabs3 = jnp.pad(qabs_s.reshape(t_s, N_HEADS, KV_LORA), ((0, 0), (0, head_rows), (0, 0)))
    qrope3 = jnp.pad(q_s.reshape(t_s, N_HEADS, HEAD_PAD)[:, :, NOPE:QK], ((0, 0), (0, head_rows), (0, 0)))
    n_pg = _tile(n_pages, 16)
    o_lat = _mla_decode(l, page_table, qabs3, qrope3, ckv_s.reshape(t_s, 1, KV_LORA),
                        kpe_s[:, NOPE:QK].reshape(t_s, 1, ROPE), kinv_s.reshape(t_s, 1, HEAD_PAD),
                        cache_mla_ckv, cache_mla_kpe, cache_mla_kinv, n_pg)
    o_mla_s = _value_up(o_lat.reshape(t_s, N_HEADS * KV_LORA), wts["w_uv"])
    o_sb_s = _sb_decode(l, page_table, sbq_s.astype(F32).reshape(t_s, N_HEADS, SB_DIM), cache_sb_k, cache_sb_v)
    o_sb_s = o_sb_s.reshape(t_s, SB_WIDTH).astype(BF16)

    base_p, h2_p = _post_attention(xp.reshape(t_p, D_MODEL), o_mla_p, o_sb_p, mods_p[2:], wts,
                                   tm=tm_p, per_token_mod=False)
    base_s, h2_s = _post_attention(xs_.reshape(t_s, D_MODEL), o_mla_s, o_sb_s, mods_s[2:], wts,
                                   tm=tm_s, per_token_mod=True)

    tm_r = tok_tile
    idx, wgt, rank, counts = _route(h2_p, h2_s, wts["w_r_hi"], wts["w_r_lo"], wts["b_router"], tm_r)
    counts = counts.reshape(N_EXPERTS).astype(I32)
    padded = (counts + EXPERT_BLOCK - 1) // EXPERT_BLOCK * EXPERT_BLOCK
    pend = jnp.cumsum(padded)
    pstart = (pend - padded).astype(I32)
    n_assign = t_all * TOP_K
    n_rows = -(-n_assign // EXPERT_BLOCK) * EXPERT_BLOCK + N_EXPERTS * EXPERT_BLOCK
    n_blocks = n_rows // EXPERT_BLOCK
    block_e = jnp.minimum(jnp.searchsorted(pend, jnp.arange(n_blocks) * EXPERT_BLOCK, side="right"),
                          N_EXPERTS - 1).astype(I32)
    n_used = (pend[-1:] // EXPERT_BLOCK).astype(I32)
    tiles3 = (ROW_CHUNKS, LANES)
    xs_rows = _dispatch(pstart, idx, rank, h2_p.reshape(t_p, *tiles3), h2_s.reshape(t_s, *tiles3), n_rows, tm_r)
    ys_rows = _experts(l, block_e, n_used, xs_rows, w_exp_gate, w_exp_up, w_exp_down)
    out3 = _combine(pstart, idx, rank, wgt, ys_rows, base_p.reshape(t_p, *tiles3), base_s.reshape(t_s, *tiles3),
                    mods_p[5].reshape(batch, *tiles3), mods_s[5].reshape(t_s, *tiles3), tm_r, seq)
    out = out3.reshape(t_all, D_MODEL)
    y_p = out[:t_p].reshape(batch, seq, D_MODEL)
    y_s = out[t_p:].reshape(nb, 1, D_MODEL)

    new_p = (ckv_p.reshape(batch, seq, KV_LORA), kpe_p[:, NOPE:QK].reshape(batch, seq, ROPE),
             kinv_p[:, :N_HEADS].reshape(batch, seq, N_HEADS),
             sbk_p.reshape(batch, seq, N_HEADS, SB_DIM), sbv_p.reshape(batch, seq, N_HEADS, SB_DIM))
    new_s = (ckv_s.reshape(nb, 1, KV_LORA), kpe_s[:, NOPE:QK].reshape(nb, 1, ROPE),
             kinv_s[:, :N_HEADS].reshape(nb, 1, N_HEADS),
             sbk_s.reshape(nb, 1, N_HEADS, SB_DIM), sbv_s.reshape(nb, 1, N_HEADS, SB_DIM))
    return y_p, y_s, new_p, new_s


def kernel(x_prompt, x_sample, cache_mla_ckv, cache_mla_kpe, cache_mla_kinv, cache_sb_k, cache_sb_v, page_table, c_prompt, c_sample, w_ada, b_ada, g_attn, g_moe, w_in, g_q_lat, w_uq, g_kv_lat, w_uk, w_uv, g_qk_q, g_qk_k, w_o, w_router, b_router, w_exp_gate, w_exp_up, w_exp_down, w_sh_gate, w_sh_up, w_sh_down):
    depth = w_in.shape[0]
    y_p, y_s = x_prompt, x_sample
    p_rows = [[] for _ in range(5)]
    s_rows = [[] for _ in range(5)]
    for l in range(depth):
        wts = _layer_weights(l, w_ada, b_ada, g_attn, g_moe, w_in, g_q_lat, w_uq, g_kv_lat, w_uk, w_uv,
                             g_qk_q, g_qk_k, w_o, w_router, b_router, w_sh_gate, w_sh_up, w_sh_down)
        y_p, y_s, new_p, new_s = _layer(l, y_p, y_s, cache_mla_ckv, cache_mla_kpe, cache_mla_kinv,
                                        cache_sb_k, cache_sb_v, page_table, c_prompt, c_sample, wts,
                                        w_exp_gate, w_exp_up, w_exp_down)
        for lst, a in zip(p_rows, new_p):
            lst.append(a)
        for lst, a in zip(s_rows, new_s):
            lst.append(a)
    outs_p = [jnp.stack(r) for r in p_rows]
    outs_s = [jnp.stack(r) for r in s_rows]
    return (y_p, y_s, *outs_p, *outs_s)
```

```python
import functools

import jax
import jax.numpy as jnp
from jax import lax
from jax.experimental import pallas as pl
from jax.experimental.pallas import tpu as pltpu

F32 = jnp.float32
BF16 = jnp.bfloat16
I32 = jnp.int32

D_MODEL = 1024
PAGE = 128
N_HEADS = 8
NOPE = 64
ROPE = 32
QK = NOPE + ROPE
HEAD_PAD = 128
Q_LORA = 768
KV_LORA = 256
SB_DIM = 64
SB_WIDTH = N_HEADS * SB_DIM
ROPE_THETA = 10000.0
MLA_SCALE = QK ** -0.5
SB_SCALE = SB_DIM ** -0.5
LOG2E = 1.4426950408889634
N_EXPERTS = 256
TOP_K = 8
N_GROUPS = 8
GROUP_SIZE = N_EXPERTS // N_GROUPS
TOPK_GROUPS = 4
D_EXPERT = 256
D_SHARED = 256
ROUTE_SCALE = 2.5
EPS = 1e-6
LANES = 128
SUBLANES = 8
ROW_CHUNKS = D_MODEL // LANES
SB_DONE = -110.0
EXPERT_BLOCK = 256
VMEM_LIMIT = 56 * 1024 * 1024

NT = (((1,), (1,)), ((), ()))


def _params(sem):
    return pltpu.CompilerParams(dimension_semantics=sem, vmem_limit_bytes=VMEM_LIMIT)


def _dot(a, b):
    return jnp.dot(a, b, preferred_element_type=F32)


def _dot_nt(a, b):
    return lax.dot_general(a, b, NT, preferred_element_type=F32)


def _split_bf16(x):
    hi = x.astype(BF16)
    lo = (x - hi.astype(F32)).astype(BF16)
    return hi, lo


def _sigmoid(x):
    return 1.0 / (1.0 + jnp.exp(-x))


def _softplus(x):
    return jnp.maximum(x, 0.0) + jnp.log(1.0 + jnp.exp(-jnp.abs(x)))


def _rms(x, g):
    return x * lax.rsqrt(jnp.mean(x * x, axis=-1, keepdims=True) + EPS) * g


def _rope128(x, cos_t, sin_t):
    lane = lax.broadcasted_iota(I32, x.shape, 1)
    up = pltpu.roll(x, LANES - ROPE // 2, 1)
    dn = pltpu.roll(x, ROPE // 2, 1)
    rot = jnp.where(lane < NOPE + ROPE // 2, up, dn)
    return x * cos_t + rot * sin_t


def _ada_kernel(c_ref, w_ref, b_ref, o_ref):
    c = c_ref[...]
    s = c * _sigmoid(c)
    o_ref[...] = _dot(s.astype(BF16), w_ref[...].astype(BF16)) + b_ref[...]


def _adaln(c, w_ada, b_ada):
    rows = c.shape[0]
    n = w_ada.shape[1]
    return pl.pallas_call(
        _ada_kernel,
        grid=(n // D_MODEL,),
        in_specs=[pl.BlockSpec((rows, D_MODEL), lambda j: (0, 0)),
                  pl.BlockSpec((D_MODEL, D_MODEL), lambda j: (0, j)),
                  pl.BlockSpec((1, D_MODEL), lambda j: (0, j))],
        out_specs=pl.BlockSpec((rows, D_MODEL), lambda j: (0, j)),
        out_shape=jax.ShapeDtypeStruct((rows, n), F32),
        compiler_params=_params(("arbitrary",)),
        name="adaln",
    )(c, w_ada, b_ada.reshape(1, n))


def _pre_kernel(x_ref, sh_ref, sc_ref, cos_ref, sin_ref, gattn_ref, gql_ref, gkv_ref, gq_ref, gkr_ref,
                win_ref, wuq_ref, wuk_ref, wuv_ref,
                q_ref, k_ref, v_ref, sbq_ref, sbkb_ref, sbvb_ref, ckv_ref, kpe_ref, kinv_ref,
                sbk_ref, sbv_ref, *maybe_qabs_ref):
    x = x_ref[...]
    h = _rms(x, gattn_ref[...]) * (1.0 + sc_ref[0]) + sh_ref[0]
    proj = _dot(h.astype(BF16), win_ref[...])
    o_kv = Q_LORA
    o_sbq = o_kv + KV_LORA
    o_sbk = o_sbq + SB_WIDTH
    o_sbv = o_sbk + SB_WIDTH
    o_kr = o_sbv + SB_WIDTH
    q_lat = proj[:, :o_kv]
    kv_lat = proj[:, o_kv:o_sbq]
    sbq = proj[:, o_sbq:o_sbk]
    sbk = proj[:, o_sbk:o_sbv]
    sbv = proj[:, o_sbv:o_kr]
    kr = proj[:, o_kr:o_kr + HEAD_PAD]

    cos_t = cos_ref[...]
    sin_t = sin_ref[...]

    sbq_ref[...] = (sbq * SB_SCALE).astype(BF16)
    sbkb_ref[...] = sbk.astype(BF16)
    sbvb_ref[...] = sbv.astype(BF16)
    sbk_ref[...] = sbk
    sbv_ref[...] = sbv

    q = _dot(_rms(q_lat, gql_ref[...]).astype(BF16), wuq_ref[...])
    gq = gq_ref[...]
    for hd in range(N_HEADS):
        qh = q[:, hd * HEAD_PAD:(hd + 1) * HEAD_PAD]
        inv = lax.rsqrt(jnp.sum(qh * qh, axis=-1, keepdims=True) * (1.0 / QK) + EPS)
        qh = _rope128(qh * inv * gq, cos_t, sin_t)
        qhb = qh.astype(BF16)
        q_ref[:, hd * HEAD_PAD:(hd + 1) * HEAD_PAD] = qhb
        if maybe_qabs_ref:
            wk_h = wuk_ref[:, hd * HEAD_PAD:(hd + 1) * HEAD_PAD]
            maybe_qabs_ref[0][:, hd * KV_LORA:(hd + 1) * KV_LORA] = _dot_nt(qhb, wk_h).astype(BF16)

    ckv = _rms(kv_lat, gkv_ref[...])
    ckv_ref[...] = ckv
    cb = ckv.astype(BF16)
    kn = _dot(cb, wuk_ref[...])
    v_ref[...] = _dot(cb, wuv_ref[...]).astype(BF16)
    kpe = _rope128(kr * gkr_ref[...], cos_t, sin_t)
    kpe_ref[...] = kpe
    lane = lax.broadcasted_iota(I32, kpe.shape, 1)
    kinv_all = jnp.zeros_like(kpe)
    for hd in range(N_HEADS):
        knh = kn[:, hd * HEAD_PAD:(hd + 1) * HEAD_PAD]
        kp = knh + kr
        inv = lax.rsqrt(jnp.sum(kp * kp, axis=-1, keepdims=True) * (1.0 / QK) + EPS)
        k_ref[:, hd * HEAD_PAD:(hd + 1) * HEAD_PAD] = ((knh + kpe) * (inv * (MLA_SCALE * LOG2E))).astype(BF16)
        kinv_all = jnp.where(lane == hd, inv, kinv_all)
    kinv_ref[...] = kinv_all


def _pre_project(x, shift, scale, cos_t, sin_t, wts, *, tm, per_token_mod, pos_tiles, with_qabs):
    t = x.shape[0]
    n_tiles = t // tm
    if per_token_mod:
        mod_spec = pl.BlockSpec((1, tm, D_MODEL), lambda i: (i, 0, 0))
    else:
        tiles_per_seq = n_tiles // shift.shape[0]
        mod_spec = pl.BlockSpec((1, 1, D_MODEL), lambda i: (i // tiles_per_seq, 0, 0))
    tab_spec = pl.BlockSpec((tm, HEAD_PAD), lambda i: (i % pos_tiles, 0))

    def full(a):
        return pl.BlockSpec(a.shape, lambda i: (0,) * a.ndim)

    def rows(width):
        return pl.BlockSpec((tm, width), lambda i: (i, 0))

    w_in_w = wts["w_in"].shape[1]
    hp = N_HEADS * HEAD_PAD
    out_shapes = [
        jax.ShapeDtypeStruct((t, hp), BF16), jax.ShapeDtypeStruct((t, hp), BF16),
        jax.ShapeDtypeStruct((t, hp), BF16),
        jax.ShapeDtypeStruct((t, SB_WIDTH), BF16), jax.ShapeDtypeStruct((t, SB_WIDTH), BF16),
        jax.ShapeDtypeStruct((t, SB_WIDTH), BF16),
        jax.ShapeDtypeStruct((t, KV_LORA), F32), jax.ShapeDtypeStruct((t, HEAD_PAD), F32),
        jax.ShapeDtypeStruct((t, HEAD_PAD), F32),
        jax.ShapeDtypeStruct((t, SB_WIDTH), F32), jax.ShapeDtypeStruct((t, SB_WIDTH), F32),
    ]
    out_specs = [rows(hp), rows(hp), rows(hp), rows(SB_WIDTH), rows(SB_WIDTH), rows(SB_WIDTH),
                 rows(KV_LORA), rows(HEAD_PAD), rows(HEAD_PAD), rows(SB_WIDTH), rows(SB_WIDTH)]
    if with_qabs:
        out_shapes.append(jax.ShapeDtypeStruct((t, N_HEADS * KV_LORA), BF16))
        out_specs.append(rows(N_HEADS * KV_LORA))
    del w_in_w
    return pl.pallas_call(
        _pre_kernel,
        grid=(n_tiles,),
        in_specs=[rows(D_MODEL), mod_spec, mod_spec, tab_spec, tab_spec,
                  full(wts["g_attn"]), full(wts["g_q_lat"]), full(wts["g_kv_lat"]), full(wts["gq"]),
                  full(wts["gkr"]), full(wts["w_in"]), full(wts["w_uq"]), full(wts["w_uk"]),
                  full(wts["w_uv"])],
        out_specs=out_specs,
        out_shape=out_shapes,
        compiler_params=_params(("arbitrary",)),
        name="pre_project",
    )(x, shift, scale, cos_t, sin_t, wts["g_attn"], wts["g_q_lat"], wts["g_kv_lat"], wts["gq"],
      wts["gkr"], wts["w_in"], wts["w_uq"], wts["w_uk"], wts["w_uv"])


def _mla_prompt_kernel(q_ref, k_ref, vt_ref, o_ref, *, blk, hg):
    qi = pl.program_id(2)
    qs = [q_ref[:, g * HEAD_PAD:(g + 1) * HEAD_PAD] for g in range(hg)]

    def step(j, carry, diagonal):
        start = pl.multiple_of(j * blk, blk)
        out = []
        for g in range(hg):
            m, l, acc = carry[g]
            k = k_ref[pl.ds(start, blk), g * HEAD_PAD:(g + 1) * HEAD_PAD]
            vt = vt_ref[j, g * HEAD_PAD:(g + 1) * HEAD_PAD, :]
            s = _dot_nt(k, qs[g])
            if diagonal:
                key = lax.broadcasted_iota(I32, s.shape, 0)
                qry = lax.broadcasted_iota(I32, s.shape, 1)
                s = jnp.where(key <= qry, s, -jnp.inf)
            m_new = jnp.maximum(m, jnp.max(s, axis=0, keepdims=True))
            alpha = jnp.exp2(m - m_new)
            p = jnp.exp2(s - m_new)
            l = alpha * l + jnp.sum(p, axis=0, keepdims=True)
            acc = alpha * acc + _dot(vt, p.astype(BF16))
            out.append((m_new, l, acc))
        return tuple(out)

    init = tuple((jnp.full((1, blk), -jnp.inf, F32), jnp.zeros((1, blk), F32),
                  jnp.zeros((HEAD_PAD, blk), F32)) for _ in range(hg))
    carry = lax.fori_loop(0, qi, lambda j, c: step(j, c, False), init)
    carry = step(qi, carry, True)
    for g in range(hg):
        _, l, acc = carry[g]
        o_ref[0, g * HEAD_PAD:(g + 1) * HEAD_PAD, :] = (acc / l).astype(BF16)


def _mla_prompt(q, k, v, batch, seq, blk, hg):
    nq = seq // blk
    width = hg * HEAD_PAD
    hp = N_HEADS * HEAD_PAD
    vt = v.reshape(batch * nq, blk, hp).transpose(0, 2, 1)
    ot = pl.pallas_call(
        functools.partial(_mla_prompt_kernel, blk=blk, hg=hg),
        grid=(batch, N_HEADS // hg, nq),
        in_specs=[pl.BlockSpec((blk, width), lambda b, h, i: (b * nq + i, h)),
                  pl.BlockSpec((seq, width), lambda b, h, i: (b, h)),
                  pl.BlockSpec((nq, width, blk), lambda b, h, i: (b, h, 0))],
        out_specs=pl.BlockSpec((1, width, blk), lambda b, h, i: (b * nq + i, h, 0)),
        out_shape=jax.ShapeDtypeStruct((batch * nq, hp, blk), BF16),
        compiler_params=_params(("arbitrary", "arbitrary", "arbitrary")),
        name="mla_prompt",
    )(q, k, vt)
    return ot.transpose(0, 2, 1).reshape(batch * seq, hp)


def _sb_prompt_kernel(q_ref, k_ref, v_ref, o_ref, *, blk):
    qi = pl.program_id(2)
    q2 = q_ref[...]
    lane = lax.broadcasted_iota(I32, q2.shape, 1)
    row = lax.broadcasted_iota(I32, (blk, blk), 0)
    col = lax.broadcasted_iota(I32, (blk, blk), 1)
    later_mat = (row > col).astype(BF16)
    zero = jnp.zeros_like(q2)
    qms = [jnp.where(lane < SB_DIM, q2, zero), jnp.where(lane >= SB_DIM, q2, zero)]

    def cond(c):
        return jnp.logical_and(c[0] >= 0, c[1] == 0)

    def body(c):
        j, _, tails, accs = c
        start = pl.multiple_of(j * blk, blk)
        k = k_ref[pl.ds(start, blk), :]
        v = v_ref[pl.ds(start, blk), :]
        mask = (col + j * blk) < (row + qi * blk)
        new_tails, new_accs = [], []
        for hh in range(2):
            z = _dot_nt(qms[hh], k)
            sp = _softplus(z)
            lf = jnp.where(mask, -sp, 0.0)
            hi, lo = _split_bf16(lf)
            later = _dot(hi, later_mat) + _dot(lo, later_mat)
            a = jnp.where(mask, jnp.exp((z - sp) + later + tails[hh]), 0.0)
            new_accs.append(accs[hh] + _dot(a.astype(BF16), v))
            new_tails.append(tails[hh] + jnp.sum(lf, axis=1, keepdims=True))
        most = jnp.max(jnp.maximum(new_tails[0], new_tails[1]))
        return j - 1, (most < SB_DONE).astype(I32), tuple(new_tails), tuple(new_accs)

    zt = jnp.zeros((blk, 1), F32)
    za = jnp.zeros((blk, 2 * SB_DIM), F32)
    accs = lax.while_loop(cond, body, (qi, jnp.int32(0), (zt, zt), (za, za)))[3]
    o_ref[...] = jnp.where(lane < SB_DIM, accs[0], accs[1]).astype(BF16)


def _sb_prompt(q, k, v, batch, seq, blk):
    nq = seq // blk
    pairs = SB_WIDTH // LANES
    return pl.pallas_call(
        functools.partial(_sb_prompt_kernel, blk=blk),
        grid=(batch, pairs, nq),
        in_specs=[pl.BlockSpec((blk, LANES), lambda b, h, i: (b * nq + i, h)),
                  pl.BlockSpec((seq, LANES), lambda b, h, i: (b, h)),
                  pl.BlockSpec((seq, LANES), lambda b, h, i: (b, h))],
        out_specs=pl.BlockSpec((blk, LANES), lambda b, h, i: (b * nq + i, h)),
        out_shape=jax.ShapeDtypeStruct(q.shape, BF16),
        compiler_params=_params(("arbitrary", "arbitrary", "arbitrary")),
        name="sb_prompt",
    )(q, k, v)


def _mla_decode_kernel(pt_ref, qabs_ref, qrope_ref, cnew_ref, penew_ref, invnew_ref, *refs, n_pg):
    del pt_ref
    ckv_refs = refs[:n_pg]
    kpe_refs = refs[n_pg:2 * n_pg]
    kinv_refs = refs[2 * n_pg:3 * n_pg]
    o_ref = refs[3 * n_pg]
    m_ref, l_ref, acc_ref = refs[3 * n_pg + 1:]
    j = pl.program_id(1)
    qa = qabs_ref[0]
    qr = qrope_ref[0]

    @pl.when(j == 0)
    def _():
        c_new = cnew_ref[0]
        s0 = (jnp.sum(qa.astype(F32)[:N_HEADS] * c_new, axis=1, keepdims=True)
              + jnp.sum(qr.astype(F32)[:N_HEADS] * penew_ref[0], axis=1, keepdims=True))
        m_ref[...] = s0 * invnew_ref[0] * MLA_SCALE
        l_ref[...] = jnp.ones_like(l_ref)
        acc_ref[...] = jnp.broadcast_to(c_new, acc_ref.shape)

    ckv = jnp.concatenate([r[0, 0] for r in ckv_refs], axis=0)
    kpe_t = jnp.concatenate([r[0, 0] for r in kpe_refs], axis=1)
    kinv_t = jnp.concatenate([r[0, 0] for r in kinv_refs], axis=1)
    cb = ckv.astype(BF16)
    s = (_dot_nt(qa, cb) + _dot(qr, kpe_t.astype(BF16)))[:N_HEADS]
    s = s * kinv_t * MLA_SCALE
    m_old = m_ref[...]
    m_new = jnp.maximum(m_old, jnp.max(s, axis=1, keepdims=True))
    alpha = jnp.exp(m_old - m_new)
    p = jnp.exp(s - m_new)
    l_ref[...] = alpha * l_ref[...] + jnp.sum(p, axis=1, keepdims=True)
    m_ref[...] = m_new
    p16 = jnp.concatenate([p, jnp.zeros_like(p)], axis=0).astype(BF16)
    acc_ref[...] = alpha * acc_ref[...] + _dot(p16, cb)[:N_HEADS]

    @pl.when(j == pl.num_programs(1) - 1)
    def _():
        o_ref[0] = acc_ref[...] / l_ref[...]


def _mla_decode(layer, page_table, qabs, qrope, c_new, pe_new, inv_new, cache_ckv, cache_kpe_t, cache_kinv_t,
                n_pg):
    nb, n_pages = page_table.shape
    steps = n_pages // n_pg
    hr = 2 * N_HEADS

    def pg(shape, p):
        return pl.BlockSpec((1, 1) + shape, lambda b, j, pt: (layer, pt[b, j * n_pg + p], 0, 0))

    in_specs = [pl.BlockSpec((1, hr, KV_LORA), lambda b, j, pt: (b, 0, 0)),
                pl.BlockSpec((1, hr, ROPE), lambda b, j, pt: (b, 0, 0)),
                pl.BlockSpec((1, 1, KV_LORA), lambda b, j, pt: (b, 0, 0)),
                pl.BlockSpec((1, 1, ROPE), lambda b, j, pt: (b, 0, 0)),
                pl.BlockSpec((1, N_HEADS, 1), lambda b, j, pt: (b, 0, 0))]
    in_specs += [pg((PAGE, KV_LORA), p) for p in range(n_pg)]
    in_specs += [pg((ROPE, PAGE), p) for p in range(n_pg)]
    in_specs += [pg((N_HEADS, PAGE), p) for p in range(n_pg)]
    return pl.pallas_call(
        functools.partial(_mla_decode_kernel, n_pg=n_pg),
        grid_spec=pltpu.PrefetchScalarGridSpec(
            num_scalar_prefetch=1,
            grid=(nb, steps),
            in_specs=in_specs,
            out_specs=pl.BlockSpec((1, N_HEADS, KV_LORA), lambda b, j, pt: (b, 0, 0)),
            scratch_shapes=[pltpu.VMEM((N_HEADS, 1), F32), pltpu.VMEM((N_HEADS, 1), F32),
                            pltpu.VMEM((N_HEADS, KV_LORA), F32)]),
        out_shape=jax.ShapeDtypeStruct((nb, N_HEADS, KV_LORA), F32),
        compiler_params=_params(("arbitrary", "arbitrary")),
        name="mla_decode",
    )(page_table, qabs, qrope, c_new, pe_new, inv_new,
      *([cache_ckv] * n_pg), *([cache_kpe_t] * n_pg), *([cache_kinv_t] * n_pg))


def _sb_page(q_ref, kt_ref, vt_ref, tail, accs):
    head = lax.broadcasted_iota(I32, (N_HEADS, PAGE), 0)
    lane = lax.broadcasted_iota(I32, (N_HEADS, PAGE), 1)
    z = jnp.zeros((N_HEADS, PAGE), F32)
    for h in range(N_HEADS):
        zh = jnp.sum(kt_ref[h] * q_ref[h], axis=0, keepdims=True)
        z = jnp.where(head == h, zh, z)
    sp = _softplus(z)
    lf = -sp
    suf = lf
    shift = 1
    while shift < PAGE:
        moved = pltpu.roll(suf, PAGE - shift, 1)
        suf = suf + jnp.where(lane < PAGE - shift, moved, 0.0)
        shift *= 2
    a = jnp.exp((z - sp) + (suf - lf) + tail)
    new_accs = []
    for h in range(N_HEADS):
        new_accs.append(accs[h] + jnp.sum(vt_ref[h] * a[h:h + 1, :], axis=1, keepdims=True))
    return tail + suf[:, 0:1], new_accs


def _sb_decode_kernel(pt_ref, q_ref, k0_ref, v0_ref, k1_ref, v1_ref, kc_ref, vc_ref, o_ref,
                      kbuf, vbuf, sem, *, layer, n_pages, n_pre):
    b = pl.program_id(0)
    q = q_ref.at[0]
    tail = jnp.zeros((N_HEADS, 1), F32)
    accs = [jnp.zeros((SB_DIM, 1), F32) for _ in range(N_HEADS)]
    tail, accs = _sb_page(q, k0_ref.at[0, 0], v0_ref.at[0, 0], tail, accs)
    if n_pre > 1:
        tail, accs = _sb_page(q, k1_ref.at[0, 0], v1_ref.at[0, 0], tail, accs)

    def cond(c):
        return jnp.logical_and(c[0] >= 0, c[1] == 0)

    def body(c):
        j, _, tail, accs = c
        page = pt_ref[b, j]
        ck = pltpu.make_async_copy(kc_ref.at[layer, page], kbuf, sem.at[0])
        cv = pltpu.make_async_copy(vc_ref.at[layer, page], vbuf, sem.at[1])
        ck.start()
        cv.start()
        ck.wait()
        cv.wait()
        tail, accs = _sb_page(q, kbuf, vbuf, tail, list(accs))
        return j - 1, (jnp.max(tail) < SB_DONE).astype(I32), tail, tuple(accs)

    done0 = (jnp.max(tail) < SB_DONE).astype(I32)
    accs = lax.while_loop(cond, body, (jnp.int32(n_pages - 1 - n_pre), done0, tail, tuple(accs)))[3]
    for h in range(N_HEADS):
        o_ref[0, h] = accs[h]


def _sb_decode(layer, page_table, q, cache_kt, cache_vt):
    nb, n_pages = page_table.shape
    n_pre = 2 if n_pages >= 2 else 1
    page_shape = (N_HEADS, SB_DIM, PAGE)

    def pg(back):
        return pl.BlockSpec((1, 1) + page_shape, lambda b, pt: (layer, pt[b, n_pages - 1 - back], 0, 0, 0))

    qspec = pl.BlockSpec((1, N_HEADS, SB_DIM, 1), lambda b, pt: (b, 0, 0, 0))
    return pl.pallas_call(
        functools.partial(_sb_decode_kernel, layer=layer, n_pages=n_pages, n_pre=n_pre),
        grid_spec=pltpu.PrefetchScalarGridSpec(
            num_scalar_prefetch=1,
            grid=(nb,),
            in_specs=[qspec, pg(0), pg(0), pg(n_pre - 1), pg(n_pre - 1),
                      pl.BlockSpec(memory_space=pl.ANY), pl.BlockSpec(memory_space=pl.ANY)],
            out_specs=qspec,
            scratch_shapes=[pltpu.VMEM(page_shape, F32), pltpu.VMEM(page_shape, F32),
                            pltpu.SemaphoreType.DMA((2,))]),
        out_shape=jax.ShapeDtypeStruct((nb, N_HEADS, SB_DIM, 1), F32),
        compiler_params=_params(("arbitrary",)),
        name="sb_decode",
    )(page_table, q, cache_kt, cache_vt, cache_kt, cache_vt, cache_kt, cache_vt)


def _uv_kernel(o_ref, w_ref, out_ref):
    for hd in range(N_HEADS):
        out_ref[:, hd * HEAD_PAD:(hd + 1) * HEAD_PAD] = _dot(
            o_ref[:, hd * KV_LORA:(hd + 1) * KV_LORA].astype(BF16), w_ref[:, hd * HEAD_PAD:(hd + 1) * HEAD_PAD]
        ).astype(BF16)


def _value_up(o_lat, w_uv_p):
    t = o_lat.shape[0]
    return pl.pallas_call(
        _uv_kernel,
        out_shape=jax.ShapeDtypeStruct((t, N_HEADS * HEAD_PAD), BF16),
        compiler_params=pltpu.CompilerParams(vmem_limit_bytes=VMEM_LIMIT),
        name="value_up",
    )(o_lat, w_uv_p)


def _post_kernel(x_ref, om_ref, os_ref, ga_ref, shm_ref, scm_ref, gm_ref, gmoe_ref, wom_ref, wos_ref,
                 wgu_ref, wd_ref, base_ref, h2_ref):
    attn = _dot(om_ref[...], wom_ref[...]) + _dot(os_ref[...], wos_ref[...])
    xm = x_ref[...] + ga_ref[0] * attn
    h2 = _rms(xm, gmoe_ref[...]) * (1.0 + scm_ref[0]) + shm_ref[0]
    h2_ref[...] = h2
    gu = _dot(h2.astype(BF16), wgu_ref[...])
    g = gu[:, :D_SHARED]
    act = (g * _sigmoid(g)) * gu[:, D_SHARED:]
    ysh = _dot(act.astype(BF16), wd_ref[...])
    base_ref[...] = xm + gm_ref[0] * ysh


def _post_attention(x, o_mla, o_sb, mods, wts, *, tm, per_token_mod):
    t = x.shape[0]
    n_tiles = t // tm
    g_a, sh_m, sc_m, g_m = mods
    if per_token_mod:
        mod_spec = pl.BlockSpec((1, tm, D_MODEL), lambda i: (i, 0, 0))
    else:
        tiles_per_seq = n_tiles // g_a.shape[0]
        mod_spec = pl.BlockSpec((1, 1, D_MODEL), lambda i: (i // tiles_per_seq, 0, 0))

    def full(a):
        return pl.BlockSpec(a.shape, lambda i: (0,) * a.ndim)

    def rows(width):
        return pl.BlockSpec((tm, width), lambda i: (i, 0))

    buf = jax.ShapeDtypeStruct((t, D_MODEL), F32)
    return pl.pallas_call(
        _post_kernel,
        grid=(n_tiles,),
        in_specs=[rows(D_MODEL), rows(N_HEADS * HEAD_PAD), rows(SB_WIDTH), mod_spec, mod_spec, mod_spec,
                  mod_spec, full(wts["g_moe"]), full(wts["w_o_mla"]), full(wts["w_o_sb"]),
                  full(wts["w_sh_gu"]), full(wts["w_sh_down"])],
        out_specs=[rows(D_MODEL), rows(D_MODEL)],
        out_shape=[buf, buf],
        compiler_params=_params(("arbitrary",)),
        name="post_attention",
    )(x, o_mla, o_sb, g_a, sh_m, sc_m, g_m, wts["g_moe"], wts["w_o_mla"], wts["w_o_sb"],
      wts["w_sh_gu"], wts["w_sh_down"])


def _router_kernel(hp_ref, hs_ref, whi_ref, wlo_ref, b_ref, idx_ref, wgt_ref, rank_ref, cnt_ref, *,
                   n_prompt_tiles):
    i = pl.program_id(0)

    @pl.when(i == 0)
    def _():
        cnt_ref[...] = jnp.zeros_like(cnt_ref)

    h = jnp.where(i < n_prompt_tiles, hp_ref[...], hs_ref[...])
    tm = h.shape[0]
    hhi, hlo = _split_bf16(h)
    whi = whi_ref[...]
    logits = _dot_nt(whi, hhi) + _dot_nt(whi, hlo) + _dot_nt(wlo_ref[...], hhi)
    scores = _sigmoid(logits)
    sel = scores + b_ref[...]
    neg = -jnp.inf

    blocks = [sel[g * GROUP_SIZE:(g + 1) * GROUP_SIZE, :] for g in range(N_GROUPS)]
    gs = []
    for blk in blocks:
        m1 = jnp.max(blk, axis=0, keepdims=True)
        is_max = blk == m1
        n_max = jnp.sum(is_max.astype(F32), axis=0, keepdims=True)
        m2 = jnp.max(jnp.where(is_max, neg, blk), axis=0, keepdims=True)
        gs.append(m1 + jnp.where(n_max >= 2.0, m1, m2))
    kept = []
    for g in range(N_GROUPS):
        ahead = jnp.zeros_like(gs[g])
        for g2 in range(N_GROUPS):
            if g2 == g:
                continue
            better = gs[g2] > gs[g]
            if g2 < g:
                better = jnp.logical_or(better, gs[g2] == gs[g])
            ahead = ahead + better.astype(F32)
        kept.append(jnp.where(ahead < float(TOPK_GROUPS), blocks[g], neg))
    cur = jnp.concatenate(kept, axis=0)

    rowf = lax.broadcasted_iota(I32, cur.shape, 0).astype(F32)
    chosen = jnp.zeros_like(cur)
    idxs, ws = [], []
    for _ in range(TOP_K):
        m = jnp.max(cur, axis=0, keepdims=True)
        ik = jnp.min(jnp.where(cur == m, rowf, float(N_EXPERTS)), axis=0, keepdims=True)
        hit = rowf == ik
        ws.append(jnp.sum(jnp.where(hit, scores, 0.0), axis=0, keepdims=True))
        cur = jnp.where(hit, neg, cur)
        chosen = chosen + hit.astype(F32)
        idxs.append(ik)
    wsum = ws[0]
    for w in ws[1:]:
        wsum = wsum + w
    ws = [w / wsum * ROUTE_SCALE for w in ws]

    r2 = lax.broadcasted_iota(I32, (tm, tm), 0)
    c2 = lax.broadcasted_iota(I32, (tm, tm), 1)
    left = (r2 < c2).astype(BF16)
    pos = _dot(chosen.astype(BF16), left) + cnt_ref[...]
    ranks = [jnp.sum(jnp.where(rowf == ik, pos, 0.0), axis=0, keepdims=True) for ik in idxs]
    cnt_ref[...] = cnt_ref[...] + jnp.sum(chosen, axis=1, keepdims=True)

    idx_ref[...] = jnp.concatenate(idxs, axis=0).astype(I32)
    wgt_ref[...] = jnp.concatenate(ws, axis=0)
    rank_ref[...] = jnp.concatenate(ranks, axis=0).astype(I32)


def _route(h2_p, h2_s, whi, wlo, b_col, tm):
    n_p = h2_p.shape[0] // tm
    t = h2_p.shape[0] + h2_s.shape[0]
    sm = pl.BlockSpec((TOP_K, tm), lambda i: (0, i))
    return pl.pallas_call(
        functools.partial(_router_kernel, n_prompt_tiles=n_p),
        grid=(t // tm,),
        in_specs=[pl.BlockSpec((tm, D_MODEL), lambda i: (jnp.minimum(i, n_p - 1), 0)),
                  pl.BlockSpec((tm, D_MODEL), lambda i: (jnp.maximum(i - n_p, 0), 0)),
                  pl.BlockSpec(whi.shape, lambda i: (0, 0)),
                  pl.BlockSpec(wlo.shape, lambda i: (0, 0)),
                  pl.BlockSpec(b_col.shape, lambda i: (0, 0))],
        out_specs=[sm, sm, sm, pl.BlockSpec((N_EXPERTS, 1), lambda i: (0, 0))],
        out_shape=[jax.ShapeDtypeStruct((TOP_K, t), I32), jax.ShapeDtypeStruct((TOP_K, t), F32),
                   jax.ShapeDtypeStruct((TOP_K, t), I32), jax.ShapeDtypeStruct((N_EXPERTS, 1), F32)],
        compiler_params=_params(("arbitrary",)),
        name="router",
    )(h2_p, h2_s, whi, wlo, b_col)


def _row(ref, r):
    return ref.at[pl.ds(r, 1), :]


def _dispatch_kernel(pstart_ref, idx_ref, rank_ref, hp_ref, hs_ref, xs_ref, sem, *, tm, n_prompt_tiles):
    i = pl.program_id(0)

    def scatter_rows(src_ref):
        def issue(t, carry):
            for k in range(TOP_K):
                dst = pstart_ref[idx_ref[k, t]] + rank_ref[k, t]
                pltpu.make_async_copy(_row(src_ref, t), _row(xs_ref, dst), sem).start()
            return carry

        lax.fori_loop(0, tm, issue, 0)

        def drain(t, carry):
            for _ in range(TOP_K):
                pltpu.make_async_copy(_row(src_ref, 0), _row(xs_ref, 0), sem).wait()
            return carry

        lax.fori_loop(0, tm, drain, 0)

    @pl.when(i < n_prompt_tiles)
    def _():
        scatter_rows(hp_ref)

    @pl.when(i >= n_prompt_tiles)
    def _():
        scatter_rows(hs_ref)


def _dispatch(pstart, idx, rank, h2_p, h2_s, n_rows, tm):
    n_p = h2_p.shape[0] // tm
    t = h2_p.shape[0] + h2_s.shape[0]
    sm = pl.BlockSpec((TOP_K, tm), lambda i, ps: (0, i), memory_space=pltpu.SMEM)
    return pl.pallas_call(
        functools.partial(_dispatch_kernel, tm=tm, n_prompt_tiles=n_p),
        grid_spec=pltpu.PrefetchScalarGridSpec(
            num_scalar_prefetch=1,
            grid=(t // tm,),
            in_specs=[sm, sm,
                      pl.BlockSpec((tm, D_MODEL), lambda i, ps: (jnp.minimum(i, n_p - 1), 0)),
                      pl.BlockSpec((tm, D_MODEL), lambda i, ps: (jnp.maximum(i - n_p, 0), 0))],
            out_specs=pl.BlockSpec(memory_space=pl.ANY),
            scratch_shapes=[pltpu.SemaphoreType.DMA(())]),
        out_shape=jax.ShapeDtypeStruct((n_rows, D_MODEL), F32),
        compiler_params=_params(("arbitrary",)),
        name="dispatch",
    )(pstart, idx, rank, h2_p, h2_s)


def _expert_kernel(be_ref, nb_ref, xs_ref, wg_ref, wu_ref, wd_ref, ys_ref, wgu_s, wd_s):
    i = pl.program_id(0)

    @pl.when(i < nb_ref[0])
    def _():
        prev = be_ref[jnp.maximum(i - 1, 0)]

        @pl.when(jnp.logical_or(i == 0, be_ref[i] != prev))
        def _():
            wgu_s[:, :D_EXPERT] = wg_ref[0].astype(BF16)
            wgu_s[:, D_EXPERT:] = wu_ref[0].astype(BF16)
            wd_s[...] = wd_ref[0].astype(BF16)

        gu = _dot(xs_ref[...].astype(BF16), wgu_s[...])
        g = gu[:, :D_EXPERT]
        act = (g * _sigmoid(g)) * gu[:, D_EXPERT:]
        ys_ref[...] = _dot(act.astype(BF16), wd_s[...])


def _experts(layer, block_e, n_used, xs, w_gate, w_up, w_down):
    n_rows = xs.shape[0]
    n_blocks = n_rows // EXPERT_BLOCK
    w_gate, w_up, w_down = (w.reshape((-1,) + w.shape[2:]) for w in (w_gate, w_up, w_down))

    def row_map(i, be, nb):
        return (jnp.minimum(i, nb[0] - 1), 0)

    def w_map(i, be, nb):
        return (layer * N_EXPERTS + be[jnp.minimum(i, nb[0] - 1)], 0, 0)

    return pl.pallas_call(
        _expert_kernel,
        grid_spec=pltpu.PrefetchScalarGridSpec(
            num_scalar_prefetch=2,
            grid=(n_blocks,),
            in_specs=[pl.BlockSpec((EXPERT_BLOCK, D_MODEL), row_map),
                      pl.BlockSpec((1, D_MODEL, D_EXPERT), w_map),
                      pl.BlockSpec((1, D_MODEL, D_EXPERT), w_map),
                      pl.BlockSpec((1, D_EXPERT, D_MODEL), w_map)],
            out_specs=pl.BlockSpec((EXPERT_BLOCK, D_MODEL), row_map),
            scratch_shapes=[pltpu.VMEM((D_MODEL, 2 * D_EXPERT), BF16), pltpu.VMEM((D_EXPERT, D_MODEL), BF16)]),
        out_shape=jax.ShapeDtypeStruct(xs.shape, F32),
        compiler_params=_params(("arbitrary",)),
        name="experts",
    )(block_e, n_used, xs, w_gate, w_up, w_down)


def _combine_kernel(pstart_ref, idx_ref, rank_ref, wgt_ref, ys_ref, basep_ref, bases_ref, gmp_ref, gms_ref,
                    outp_ref, outs_ref, buf, sem, *, tm, n_prompt_tiles):
    i = pl.program_id(0)

    def issue(t, carry):
        for k in range(TOP_K):
            src = pstart_ref[idx_ref[k, t]] + rank_ref[k, t]
            pltpu.make_async_copy(_row(ys_ref, src), _row(buf.at[k], t), sem).start()
        return carry

    lax.fori_loop(0, tm, issue, 0)

    def drain(t, carry):
        for k in range(TOP_K):
            pltpu.make_async_copy(_row(ys_ref, 0), _row(buf.at[k], 0), sem).wait()
        return carry

    lax.fori_loop(0, tm, drain, 0)

    w = wgt_ref[...]
    mix = w[:, 0:1] * buf[0]
    for k in range(1, TOP_K):
        mix = mix + w[:, k:k + 1] * buf[k]

    @pl.when(i < n_prompt_tiles)
    def _():
        outp_ref[...] = basep_ref[...] + gmp_ref[0] * mix

    @pl.when(i >= n_prompt_tiles)
    def _():
        outs_ref[...] = bases_ref[...] + gms_ref[...] * mix


def _combine(pstart, idx, rank, wgt_t, ys, base_p, base_s, gm_prompt, gm_sample, tm, seq):
    t_all = base_p.shape[0] + base_s.shape[0]
    n_batch = gm_prompt.shape[0]
    n_prompt_tiles = n_batch * seq // tm
    tiles_per_seq = seq // tm
    sm = pl.BlockSpec((TOP_K, tm), lambda i, ps: (0, i), memory_space=pltpu.SMEM)
    tile = (tm, D_MODEL)
    prompt_tile = pl.BlockSpec(tile, lambda i, ps: (jnp.minimum(i, n_prompt_tiles - 1), 0))
    sample_tile = pl.BlockSpec(tile, lambda i, ps: (jnp.maximum(i - n_prompt_tiles, 0), 0))
    return pl.pallas_call(
        functools.partial(_combine_kernel, tm=tm, n_prompt_tiles=n_prompt_tiles),
        grid_spec=pltpu.PrefetchScalarGridSpec(
            num_scalar_prefetch=1,
            grid=(t_all // tm,),
            in_specs=[sm, sm, pl.BlockSpec((tm, TOP_K), lambda i, ps: (i, 0)),
                      pl.BlockSpec(memory_space=pl.ANY), prompt_tile, sample_tile,
                      pl.BlockSpec((1, 1, D_MODEL),
                                   lambda i, ps: (jnp.minimum(i // tiles_per_seq, n_batch - 1), 0, 0)),
                      sample_tile],
            out_specs=[prompt_tile, sample_tile],
            scratch_shapes=[pltpu.VMEM((TOP_K, tm, D_MODEL), F32), pltpu.SemaphoreType.DMA(())]),
        out_shape=[jax.ShapeDtypeStruct(base_p.shape, F32), jax.ShapeDtypeStruct(base_s.shape, F32)],
        compiler_params=_params(("arbitrary",)),
        name="combine",
    )(pstart, idx, rank, wgt_t, ys, base_p, base_s, gm_prompt, gm_sample)


def _pad_heads(w, width):
    k = w.shape[0]
    w = w.reshape(k, N_HEADS, width)
    return jnp.pad(w, ((0, 0), (0, 0), (0, HEAD_PAD - width))).reshape(k, N_HEADS * HEAD_PAD)


def _rope_tables(pos):
    half = ROPE // 2
    inv_freq = ROPE_THETA ** (-jnp.arange(half, dtype=F32) * (2.0 / ROPE))
    ang = pos.astype(F32)[:, None] * inv_freq[None, :]
    cos, sin = jnp.cos(ang), jnp.sin(ang)
    n = pos.shape[0]
    ones = jnp.ones((n, NOPE), F32)
    zeros = jnp.zeros((n, HEAD_PAD - QK), F32)
    cos_t = jnp.concatenate([ones, cos, cos, zeros], axis=1)
    sin_t = jnp.concatenate([jnp.zeros((n, NOPE), F32), -sin, sin, zeros], axis=1)
    return cos_t, sin_t


def _layer_weights(l, w_ada, b_ada, g_attn, g_moe, w_in, g_q_lat, w_uq, g_kv_lat, w_uk, w_uv, g_qk_q, g_qk_k,
                   w_o, w_router, b_router, w_sh_gate, w_sh_up, w_sh_down):
    wi = w_in[l]
    o1 = Q_LORA + KV_LORA
    o2 = o1 + ROPE
    kr_block = jnp.zeros((D_MODEL, HEAD_PAD), F32).at[:, NOPE:QK].set(wi[:, o1:o2])
    w_in_p = jnp.concatenate([wi[:, :o1], wi[:, o2:], kr_block], axis=1).astype(BF16)
    gq = jnp.concatenate([g_qk_q[l][:NOPE] * g_qk_k[l][:NOPE], g_qk_q[l][NOPE:],
                          jnp.zeros((HEAD_PAD - QK,), F32)]).reshape(1, HEAD_PAD)
    gkr = jnp.concatenate([jnp.zeros((NOPE,), F32), g_qk_k[l][NOPE:],
                           jnp.zeros((HEAD_PAD - QK,), F32)]).reshape(1, HEAD_PAD)
    wo = w_o[l]
    n_mla = N_HEADS * NOPE
    w_o_mla = jnp.pad(wo[:n_mla].reshape(N_HEADS, NOPE, D_MODEL),
                      ((0, 0), (0, HEAD_PAD - NOPE), (0, 0))).reshape(N_HEADS * HEAD_PAD, D_MODEL)
    wr_t = w_router[l].T
    whi = wr_t.astype(BF16)
    wlo = (wr_t - whi.astype(F32)).astype(BF16)
    return dict(
        w_ada=w_ada[l], b_ada=b_ada[l],
        g_attn=g_attn[l].reshape(1, -1), g_moe=g_moe[l].reshape(1, -1),
        g_q_lat=g_q_lat[l].reshape(1, -1), g_kv_lat=g_kv_lat[l].reshape(1, -1), gq=gq, gkr=gkr,
        w_in=w_in_p, w_uq=_pad_heads(w_uq[l], QK).astype(BF16),
        w_uk=_pad_heads(w_uk[l], NOPE).astype(BF16), w_uv=_pad_heads(w_uv[l], NOPE).astype(BF16),
        w_o_mla=w_o_mla.astype(BF16), w_o_sb=wo[n_mla:].astype(BF16),
        w_r_hi=whi, w_r_lo=wlo, b_router=b_router[l].reshape(-1, 1),
        w_sh_gu=jnp.concatenate([w_sh_gate[l], w_sh_up[l]], axis=1).astype(BF16),
        w_sh_down=w_sh_down[l].astype(BF16),
    )


def _tile(n, pref):
    t = min(n, pref)
    while n % t:
        t //= 2
    return t


def _layer(l, xp, xs_, cache_mla_ckv, cache_mla_kpe, cache_mla_kinv, cache_sb_k, cache_sb_v, page_table,
           c_prompt, c_sample, wts, w_exp_gate, w_exp_up, w_exp_down):
    batch, seq, _ = xp.shape
    nb = xs_.shape[0]
    n_pages = page_table.shape[1]
    past = n_pages * PAGE
    t_p = batch * seq
    tok_tile = LANES
    assert xs_.shape[1] == 1 and nb % tok_tile == 0 and seq % tok_tile == 0
    t_s = nb
    tm_s = tok_tile
    t_all = t_p + t_s

    mods = _adaln(jnp.concatenate([c_prompt, c_sample], axis=0), wts["w_ada"], wts["b_ada"])
    mods_p = [m.reshape(batch, 1, D_MODEL) for m in jnp.split(mods[:batch], 6, axis=-1)]
    mods_s = [m.reshape(t_s // tm_s, tm_s, D_MODEL) for m in jnp.split(mods[batch:], 6, axis=-1)]

    tm_p = _tile(seq, 512)
    cos_p, sin_p = _rope_tables(jnp.arange(seq))
    (q, k, v, sbq, sbkb, sbvb, ckv_p, kpe_p, kinv_p, sbk_p, sbv_p) = _pre_project(
        xp.reshape(t_p, D_MODEL), mods_p[0], mods_p[1], cos_p, sin_p, wts,
        tm=tm_p, per_token_mod=False, pos_tiles=seq // tm_p, with_qabs=False)
    o_mla_p = _mla_prompt(q, k, v, batch, seq, _tile(seq, 512), 2)
    o_sb_p = _sb_prompt(sbq, sbkb, sbvb, batch, seq, _tile(seq, 256))

    cos_s, sin_s = _rope_tables(jnp.full((tm_s,), past, I32))
    (q_s, _, _, sbq_s, _, _, ckv_s, kpe_s, kinv_s, sbk_s, sbv_s, qabs_s) = _pre_project(
        xs_.reshape(t_s, D_MODEL), mods_s[0], mods_s[1], cos_s, sin_s, wts,
        tm=tm_s, per_token_mod=True, pos_tiles=1, with_qabs=True)
    head_rows = N_HEADS
    qabs3 = jnp.pad(qabs_s.reshape(t_s, N_HEADS, KV_LORA), ((0, 0), (0, head_rows), (0, 0)))
    qrope3 = jnp.pad(q_s.reshape(t_s, N_HEADS, HEAD_PAD)[:, :, NOPE:QK], ((0, 0), (0, head_rows), (0, 0)))
    n_pg = _tile(n_pages, 16)
    o_lat = _mla_decode(l, page_table, qabs3, qrope3, ckv_s.reshape(t_s, 1, KV_LORA),
                        kpe_s[:, NOPE:QK].reshape(t_s, 1, ROPE), kinv_s[:, :N_HEADS].reshape(t_s, N_HEADS, 1),
                        cache_mla_ckv, jnp.swapaxes(cache_mla_kpe, 2, 3), jnp.swapaxes(cache_mla_kinv, 2, 3),
                        n_pg)
    o_mla_s = _value_up(o_lat.reshape(t_s, N_HEADS * KV_LORA), wts["w_uv"])
    o_sb_s = _sb_decode(l, page_table, sbq_s.astype(F32).reshape(t_s, N_HEADS, SB_DIM, 1),
                        cache_sb_k.transpose(0, 1, 3, 4, 2), cache_sb_v.transpose(0, 1, 3, 4, 2))
    o_sb_s = o_sb_s.reshape(t_s, SB_WIDTH).astype(BF16)

    base_p, h2_p = _post_attention(xp.reshape(t_p, D_MODEL), o_mla_p, o_sb_p, mods_p[2:], wts,
                                   tm=tm_p, per_token_mod=False)
    base_s, h2_s = _post_attention(xs_.reshape(t_s, D_MODEL), o_mla_s, o_sb_s, mods_s[2:], wts,
                                   tm=tm_s, per_token_mod=True)

    tm_r = tok_tile
    idx, wgt, rank, counts = _route(h2_p, h2_s, wts["w_r_hi"], wts["w_r_lo"], wts["b_router"], tm_r)
    counts = counts.reshape(N_EXPERTS).astype(I32)
    padded = (counts + EXPERT_BLOCK - 1) // EXPERT_BLOCK * EXPERT_BLOCK
    pend = jnp.cumsum(padded)
    pstart = (pend - padded).astype(I32)
    n_assign = t_all * TOP_K
    n_rows = -(-n_assign // EXPERT_BLOCK) * EXPERT_BLOCK + N_EXPERTS * EXPERT_BLOCK
    n_blocks = n_rows // EXPERT_BLOCK
    block_e = jnp.minimum(jnp.searchsorted(pend, jnp.arange(n_blocks) * EXPERT_BLOCK, side="right"),
                          N_EXPERTS - 1).astype(I32)
    n_used = (pend[-1:] // EXPERT_BLOCK).astype(I32)
    xs_rows = _dispatch(pstart, idx, rank, h2_p, h2_s, n_rows, tm_r)
    ys_rows = _experts(l, block_e, n_used, xs_rows, w_exp_gate, w_exp_up, w_exp_down)
    y_p, y_s = _combine(pstart, idx, rank, wgt.T, ys_rows, base_p, base_s, mods_p[5],
                        mods_s[5].reshape(t_s, D_MODEL), tm_r, seq)
    y_p = y_p.reshape(batch, seq, D_MODEL)
    y_s = y_s.reshape(nb, 1, D_MODEL)

    new_p = (ckv_p.reshape(batch, seq, KV_LORA), kpe_p[:, NOPE:QK].reshape(batch, seq, ROPE),
             kinv_p[:, :N_HEADS].reshape(batch, seq, N_HEADS),
             sbk_p.reshape(batch, seq, N_HEADS, SB_DIM), sbv_p.reshape(batch, seq, N_HEADS, SB_DIM))
    new_s = (ckv_s.reshape(nb, 1, KV_LORA), kpe_s[:, NOPE:QK].reshape(nb, 1, ROPE),
             kinv_s[:, :N_HEADS].reshape(nb, 1, N_HEADS),
             sbk_s.reshape(nb, 1, N_HEADS, SB_DIM), sbv_s.reshape(nb, 1, N_HEADS, SB_DIM))
    return y_p, y_s, new_p, new_s


def kernel(x_prompt, x_sample, cache_mla_ckv, cache_mla_kpe, cache_mla_kinv, cache_sb_k, cache_sb_v, page_table, c_prompt, c_sample, w_ada, b_ada, g_attn, g_moe, w_in, g_q_lat, w_uq, g_kv_lat, w_uk, w_uv, g_qk_q, g_qk_k, w_o, w_router, b_router, w_exp_gate, w_exp_up, w_exp_down, w_sh_gate, w_sh_up, w_sh_down):
    depth = w_in.shape[0]
    y_p, y_s = x_prompt, x_sample
    p_rows = [[] for _ in range(5)]
    s_rows = [[] for _ in range(5)]
    for l in range(depth):
        wts = _layer_weights(l, w_ada, b_ada, g_attn, g_moe, w_in, g_q_lat, w_uq, g_kv_lat, w_uk, w_uv,
                             g_qk_q, g_qk_k, w_o, w_router, b_router, w_sh_gate, w_sh_up, w_sh_down)
        y_p, y_s, new_p, new_s = _layer(l, y_p, y_s, cache_mla_ckv, cache_mla_kpe, cache_mla_kinv,
                                        cache_sb_k, cache_sb_v, page_table, c_prompt, c_sample, wts,
                                        w_exp_gate, w_exp_up, w_exp_down)
        for lst, a in zip(p_rows, new_p):
            lst.append(a)
        for lst, a in zip(s_rows, new_s):
            lst.append(a)
    outs_p = [jnp.stack(r) for r in p_rows]
    outs_s = [jnp.stack(r) for r in s_rows]
    return (y_p, y_s, *outs_p, *outs_s)
```

```python
import functools

import jax
import jax.numpy as jnp
from jax import lax
from jax.experimental import pallas as pl
from jax.experimental.pallas import tpu as pltpu

F32 = jnp.float32
BF16 = jnp.bfloat16
I32 = jnp.int32

D_MODEL = 1024
PAGE = 128
N_HEADS = 8
NOPE = 64
ROPE = 32
QK = NOPE + ROPE
HEAD_PAD = 128
Q_LORA = 768
KV_LORA = 256
SB_DIM = 64
SB_WIDTH = N_HEADS * SB_DIM
ROPE_THETA = 10000.0
MLA_SCALE = QK ** -0.5
SB_SCALE = SB_DIM ** -0.5
LOG2E = 1.4426950408889634
N_EXPERTS = 256
TOP_K = 8
N_GROUPS = 8
GROUP_SIZE = N_EXPERTS // N_GROUPS
TOPK_GROUPS = 4
D_EXPERT = 256
D_SHARED = 256
ROUTE_SCALE = 2.5
EPS = 1e-6
LANES = 128
SUBLANES = 8
ROW_CHUNKS = D_MODEL // LANES
SB_DONE = -110.0
EXPERT_BLOCK = 256
VMEM_LIMIT = 56 * 1024 * 1024

NT = (((1,), (1,)), ((), ()))


def _params(sem):
    return pltpu.CompilerParams(dimension_semantics=sem, vmem_limit_bytes=VMEM_LIMIT)


def _dot(a, b):
    return jnp.dot(a, b, preferred_element_type=F32)


def _dot_nt(a, b):
    return lax.dot_general(a, b, NT, preferred_element_type=F32)


def _split_bf16(x):
    hi = x.astype(BF16)
    lo = (x - hi.astype(F32)).astype(BF16)
    return hi, lo


def _sigmoid(x):
    return 1.0 / (1.0 + jnp.exp(-x))


def _softplus(x):
    return jnp.maximum(x, 0.0) + jnp.log(1.0 + jnp.exp(-jnp.abs(x)))


def _rms(x, g):
    return x * lax.rsqrt(jnp.mean(x * x, axis=-1, keepdims=True) + EPS) * g


def _rope128(x, cos_t, sin_t):
    lane = lax.broadcasted_iota(I32, x.shape, 1)
    up = pltpu.roll(x, LANES - ROPE // 2, 1)
    dn = pltpu.roll(x, ROPE // 2, 1)
    rot = jnp.where(lane < NOPE + ROPE // 2, up, dn)
    return x * cos_t + rot * sin_t


def _ada_kernel(c_ref, w_ref, b_ref, o_ref):
    c = c_ref[...]
    s = c * _sigmoid(c)
    o_ref[...] = _dot(s.astype(BF16), w_ref[...].astype(BF16)) + b_ref[...]


def _adaln(c, w_ada, b_ada):
    rows = c.shape[0]
    n = w_ada.shape[1]
    return pl.pallas_call(
        _ada_kernel,
        grid=(n // D_MODEL,),
        in_specs=[pl.BlockSpec((rows, D_MODEL), lambda j: (0, 0)),
                  pl.BlockSpec((D_MODEL, D_MODEL), lambda j: (0, j)),
                  pl.BlockSpec((1, D_MODEL), lambda j: (0, j))],
        out_specs=pl.BlockSpec((rows, D_MODEL), lambda j: (0, j)),
        out_shape=jax.ShapeDtypeStruct((rows, n), F32),
        compiler_params=_params(("arbitrary",)),
        name="adaln",
    )(c, w_ada, b_ada.reshape(1, n))


def _pre_kernel(x_ref, sh_ref, sc_ref, cos_ref, sin_ref, gattn_ref, gql_ref, gkv_ref, gq_ref, gkr_ref,
                win_ref, wuq_ref, wuk_ref, wuv_ref,
                q_ref, k_ref, v_ref, sbq_ref, sbkb_ref, sbvb_ref, ckv_ref, kpe_ref, kinv_ref,
                sbk_ref, sbv_ref, *maybe_qabs_ref):
    tm = x_ref.shape[0]
    n_split = 2 if tm >= 2 * LANES else 1
    for part in range(n_split):
        rs = slice(part * (tm // n_split), (part + 1) * (tm // n_split))
        _pre_rows(rs, x_ref, sh_ref, sc_ref, cos_ref, sin_ref, gattn_ref, gql_ref, gkv_ref, gq_ref, gkr_ref,
                  win_ref, wuq_ref, wuk_ref, wuv_ref,
                  q_ref, k_ref, v_ref, sbq_ref, sbkb_ref, sbvb_ref, ckv_ref, kpe_ref, kinv_ref,
                  sbk_ref, sbv_ref, maybe_qabs_ref)


def _pre_rows(rs, x_ref, sh_ref, sc_ref, cos_ref, sin_ref, gattn_ref, gql_ref, gkv_ref, gq_ref, gkr_ref,
              win_ref, wuq_ref, wuk_ref, wuv_ref,
              q_ref, k_ref, v_ref, sbq_ref, sbkb_ref, sbvb_ref, ckv_ref, kpe_ref, kinv_ref,
              sbk_ref, sbv_ref, maybe_qabs_ref):
    def mod(ref):
        return ref[0] if ref.shape[1] == 1 else ref[0, rs, :]

    x = x_ref[rs, :]
    h = _rms(x, gattn_ref[...]) * (1.0 + mod(sc_ref)) + mod(sh_ref)
    proj = _dot(h.astype(BF16), win_ref[...])
    o_kv = Q_LORA
    o_sbq = o_kv + KV_LORA
    o_sbk = o_sbq + SB_WIDTH
    o_sbv = o_sbk + SB_WIDTH
    o_kr = o_sbv + SB_WIDTH
    q_lat = proj[:, :o_kv]
    kv_lat = proj[:, o_kv:o_sbq]
    sbq = proj[:, o_sbq:o_sbk]
    sbk = proj[:, o_sbk:o_sbv]
    sbv = proj[:, o_sbv:o_kr]
    kr = proj[:, o_kr:o_kr + HEAD_PAD]

    cos_t = cos_ref[rs, :]
    sin_t = sin_ref[rs, :]

    sbq_ref[rs, :] = (sbq * SB_SCALE).astype(BF16)
    sbkb_ref[rs, :] = sbk.astype(BF16)
    sbvb_ref[rs, :] = sbv.astype(BF16)
    sbk_ref[rs, :] = sbk
    sbv_ref[rs, :] = sbv

    q = _dot(_rms(q_lat, gql_ref[...]).astype(BF16), wuq_ref[...])
    gq = gq_ref[...]
    for hd in range(N_HEADS):
        qh = q[:, hd * HEAD_PAD:(hd + 1) * HEAD_PAD]
        inv = lax.rsqrt(jnp.sum(qh * qh, axis=-1, keepdims=True) * (1.0 / QK) + EPS)
        qh = _rope128(qh * inv * gq, cos_t, sin_t)
        qhb = qh.astype(BF16)
        q_ref[rs, hd * HEAD_PAD:(hd + 1) * HEAD_PAD] = qhb
        if maybe_qabs_ref:
            wk_h = wuk_ref[:, hd * HEAD_PAD:(hd + 1) * HEAD_PAD]
            maybe_qabs_ref[0][rs, hd * KV_LORA:(hd + 1) * KV_LORA] = _dot_nt(qhb, wk_h).astype(BF16)

    ckv = _rms(kv_lat, gkv_ref[...])
    ckv_ref[rs, :] = ckv
    cb = ckv.astype(BF16)
    kn = _dot(cb, wuk_ref[...])
    v_ref[rs, :] = _dot(cb, wuv_ref[...]).astype(BF16)
    kpe = _rope128(kr * gkr_ref[...], cos_t, sin_t)
    kpe_ref[rs, :] = kpe
    lane = lax.broadcasted_iota(I32, kpe.shape, 1)
    kinv_all = jnp.zeros_like(kpe)
    for hd in range(N_HEADS):
        knh = kn[:, hd * HEAD_PAD:(hd + 1) * HEAD_PAD]
        kp = knh + kr
        inv = lax.rsqrt(jnp.sum(kp * kp, axis=-1, keepdims=True) * (1.0 / QK) + EPS)
        k_ref[rs, hd * HEAD_PAD:(hd + 1) * HEAD_PAD] = ((knh + kpe) * (inv * (MLA_SCALE * LOG2E))).astype(BF16)
        kinv_all = jnp.where(lane == hd, inv, kinv_all)
    kinv_ref[rs, :] = kinv_all


def _pre_project(x, shift, scale, cos_t, sin_t, wts, *, tm, per_token_mod, pos_tiles, with_qabs):
    t = x.shape[0]
    n_tiles = t // tm
    if per_token_mod:
        mod_spec = pl.BlockSpec((1, tm, D_MODEL), lambda i: (i, 0, 0))
    else:
        tiles_per_seq = n_tiles // shift.shape[0]
        mod_spec = pl.BlockSpec((1, 1, D_MODEL), lambda i: (i // tiles_per_seq, 0, 0))
    tab_spec = pl.BlockSpec((tm, HEAD_PAD), lambda i: (i % pos_tiles, 0))

    def full(a):
        return pl.BlockSpec(a.shape, lambda i: (0,) * a.ndim)

    def rows(width):
        return pl.BlockSpec((tm, width), lambda i: (i, 0))

    w_in_w = wts["w_in"].shape[1]
    hp = N_HEADS * HEAD_PAD
    out_shapes = [
        jax.ShapeDtypeStruct((t, hp), BF16), jax.ShapeDtypeStruct((t, hp), BF16),
        jax.ShapeDtypeStruct((t, hp), BF16),
        jax.ShapeDtypeStruct((t, SB_WIDTH), BF16), jax.ShapeDtypeStruct((t, SB_WIDTH), BF16),
        jax.ShapeDtypeStruct((t, SB_WIDTH), BF16),
        jax.ShapeDtypeStruct((t, KV_LORA), F32), jax.ShapeDtypeStruct((t, HEAD_PAD), F32),
        jax.ShapeDtypeStruct((t, HEAD_PAD), F32),
        jax.ShapeDtypeStruct((t, SB_WIDTH), F32), jax.ShapeDtypeStruct((t, SB_WIDTH), F32),
    ]
    out_specs = [rows(hp), rows(hp), rows(hp), rows(SB_WIDTH), rows(SB_WIDTH), rows(SB_WIDTH),
                 rows(KV_LORA), rows(HEAD_PAD), rows(HEAD_PAD), rows(SB_WIDTH), rows(SB_WIDTH)]
    if with_qabs:
        out_shapes.append(jax.ShapeDtypeStruct((t, N_HEADS * KV_LORA), BF16))
        out_specs.append(rows(N_HEADS * KV_LORA))
    del w_in_w
    return pl.pallas_call(
        _pre_kernel,
        grid=(n_tiles,),
        in_specs=[rows(D_MODEL), mod_spec, mod_spec, tab_spec, tab_spec,
                  full(wts["g_attn"]), full(wts["g_q_lat"]), full(wts["g_kv_lat"]), full(wts["gq"]),
                  full(wts["gkr"]), full(wts["w_in"]), full(wts["w_uq"]), full(wts["w_uk"]),
                  full(wts["w_uv"])],
        out_specs=out_specs,
        out_shape=out_shapes,
        compiler_params=_params(("arbitrary",)),
        name="pre_project",
    )(x, shift, scale, cos_t, sin_t, wts["g_attn"], wts["g_q_lat"], wts["g_kv_lat"], wts["gq"],
      wts["gkr"], wts["w_in"], wts["w_uq"], wts["w_uk"], wts["w_uv"])


def _mla_prompt_kernel(q_ref, k_ref, vt_ref, o_ref, *, blk, hg):
    qi = pl.program_id(2)
    qs = [q_ref[:, g * HEAD_PAD:(g + 1) * HEAD_PAD] for g in range(hg)]

    def step(j, carry, diagonal):
        start = pl.multiple_of(j * blk, blk)
        out = []
        for g in range(hg):
            m, l, acc = carry[g]
            k = k_ref[pl.ds(start, blk), g * HEAD_PAD:(g + 1) * HEAD_PAD]
            vt = vt_ref[j, g * HEAD_PAD:(g + 1) * HEAD_PAD, :]
            s = _dot_nt(k, qs[g])
            if diagonal:
                key = lax.broadcasted_iota(I32, s.shape, 0)
                qry = lax.broadcasted_iota(I32, s.shape, 1)
                s = jnp.where(key <= qry, s, -jnp.inf)
            m_new = jnp.maximum(m, jnp.max(s, axis=0, keepdims=True))
            alpha = jnp.exp2(m - m_new)
            p = jnp.exp2(s - m_new)
            l = alpha * l + jnp.sum(p, axis=0, keepdims=True)
            acc = alpha * acc + _dot(vt, p.astype(BF16))
            out.append((m_new, l, acc))
        return tuple(out)

    init = tuple((jnp.full((1, blk), -jnp.inf, F32), jnp.zeros((1, blk), F32),
                  jnp.zeros((HEAD_PAD, blk), F32)) for _ in range(hg))
    carry = lax.fori_loop(0, qi, lambda j, c: step(j, c, False), init)
    carry = step(qi, carry, True)
    for g in range(hg):
        _, l, acc = carry[g]
        o_ref[0, g * HEAD_PAD:(g + 1) * HEAD_PAD, :] = (acc / l).astype(BF16)


def _mla_prompt(q, k, v, batch, seq, blk, hg):
    nq = seq // blk
    width = hg * HEAD_PAD
    hp = N_HEADS * HEAD_PAD
    vt = v.reshape(batch * nq, blk, hp).transpose(0, 2, 1)
    ot = pl.pallas_call(
        functools.partial(_mla_prompt_kernel, blk=blk, hg=hg),
        grid=(batch, N_HEADS // hg, nq),
        in_specs=[pl.BlockSpec((blk, width), lambda b, h, i: (b * nq + i, h)),
                  pl.BlockSpec((seq, width), lambda b, h, i: (b, h)),
                  pl.BlockSpec((nq, width, blk), lambda b, h, i: (b, h, 0))],
        out_specs=pl.BlockSpec((1, width, blk), lambda b, h, i: (b * nq + i, h, 0)),
        out_shape=jax.ShapeDtypeStruct((batch * nq, hp, blk), BF16),
        compiler_params=_params(("arbitrary", "arbitrary", "arbitrary")),
        name="mla_prompt",
    )(q, k, vt)
    return ot.transpose(0, 2, 1).reshape(batch * seq, hp)


def _sb_prompt_kernel(q_ref, k_ref, v_ref, o_ref, *, blk):
    qi = pl.program_id(2)
    q2 = q_ref[...]
    lane = lax.broadcasted_iota(I32, q2.shape, 1)
    row = lax.broadcasted_iota(I32, (blk, blk), 0)
    col = lax.broadcasted_iota(I32, (blk, blk), 1)
    later_mat = (row > col).astype(BF16)
    zero = jnp.zeros_like(q2)
    qms = [jnp.where(lane < SB_DIM, q2, zero), jnp.where(lane >= SB_DIM, q2, zero)]

    def cond(c):
        return jnp.logical_and(c[0] >= 0, c[1] == 0)

    def body(c):
        j, _, tails, accs = c
        start = pl.multiple_of(j * blk, blk)
        k = k_ref[pl.ds(start, blk), :]
        v = v_ref[pl.ds(start, blk), :]
        mask = (col + j * blk) < (row + qi * blk)
        new_tails, new_accs = [], []
        for hh in range(2):
            z = _dot_nt(qms[hh], k)
            sp = _softplus(z)
            lf = jnp.where(mask, -sp, 0.0)
            hi, lo = _split_bf16(lf)
            later = _dot(hi, later_mat) + _dot(lo, later_mat)
            a = jnp.where(mask, jnp.exp((z - sp) + later + tails[hh]), 0.0)
            new_accs.append(accs[hh] + _dot(a.astype(BF16), v))
            new_tails.append(tails[hh] + jnp.sum(lf, axis=1, keepdims=True))
        most = jnp.max(jnp.maximum(new_tails[0], new_tails[1]))
        return j - 1, (most < SB_DONE).astype(I32), tuple(new_tails), tuple(new_accs)

    zt = jnp.zeros((blk, 1), F32)
    za = jnp.zeros((blk, 2 * SB_DIM), F32)
    accs = lax.while_loop(cond, body, (qi, jnp.int32(0), (zt, zt), (za, za)))[3]
    o_ref[...] = jnp.where(lane < SB_DIM, accs[0], accs[1]).astype(BF16)


def _sb_prompt(q, k, v, batch, seq, blk):
    nq = seq // blk
    pairs = SB_WIDTH // LANES
    return pl.pallas_call(
        functools.partial(_sb_prompt_kernel, blk=blk),
        grid=(batch, pairs, nq),
        in_specs=[pl.BlockSpec((blk, LANES), lambda b, h, i: (b * nq + i, h)),
                  pl.BlockSpec((seq, LANES), lambda b, h, i: (b, h)),
                  pl.BlockSpec((seq, LANES), lambda b, h, i: (b, h))],
        out_specs=pl.BlockSpec((blk, LANES), lambda b, h, i: (b * nq + i, h)),
        out_shape=jax.ShapeDtypeStruct(q.shape, BF16),
        compiler_params=_params(("arbitrary", "arbitrary", "arbitrary")),
        name="sb_prompt",
    )(q, k, v)


def _mla_decode_kernel(pt_ref, qabs_ref, qrope_ref, cnew_ref, penew_ref, invnew_ref,
                       ckv_hbm, kpe_hbm, kinv_hbm, o_ref, ckv_buf, kpe_buf, kinv_buf, sem,
                       *, layer, n_pg, n_chunks):
    b = pl.program_id(0)
    nb = pl.num_programs(0)
    qa = qabs_ref[0]
    qr = qrope_ref[0]

    def copies(seq, chunk, slot):
        out = []
        for p in range(n_pg):
            page = pt_ref[seq, chunk * n_pg + p]
            out.append(pltpu.make_async_copy(ckv_hbm.at[layer, page], ckv_buf.at[slot, p], sem.at[slot, 0]))
            out.append(pltpu.make_async_copy(kpe_hbm.at[layer, page], kpe_buf.at[slot, p], sem.at[slot, 1]))
            out.append(pltpu.make_async_copy(kinv_hbm.at[layer, page], kinv_buf.at[slot, p], sem.at[slot, 2]))
        return out

    def start(seq, chunk, slot):
        for c in copies(seq, chunk, slot):
            c.start()

    def wait(slot):
        for c in copies(0, 0, slot):
            c.wait()

    @pl.when(b == 0)
    def _():
        start(0, 0, 0)

    c_new = cnew_ref[0]
    s0 = (jnp.sum(qa.astype(F32)[:N_HEADS] * c_new, axis=1, keepdims=True)
          + jnp.sum(qr.astype(F32)[:N_HEADS] * penew_ref[0], axis=1, keepdims=True))
    init = (s0 * invnew_ref[0] * MLA_SCALE, jnp.ones((N_HEADS, 1), F32),
            jnp.broadcast_to(c_new, (N_HEADS, KV_LORA)))

    def chunk_step(slot, carry):
        m_old, l, acc = carry
        cb = ckv_buf[slot].reshape(n_pg * PAGE, KV_LORA).astype(BF16)
        kpe_t = jnp.concatenate([kpe_buf[slot, p] for p in range(n_pg)], axis=1)
        kinv_t = jnp.concatenate([kinv_buf[slot, p] for p in range(n_pg)], axis=1)
        s = (_dot_nt(qa, cb) + _dot(qr, kpe_t.astype(BF16)))[:N_HEADS]
        s = s * kinv_t * MLA_SCALE
        m_new = jnp.maximum(m_old, jnp.max(s, axis=1, keepdims=True))
        alpha = jnp.exp(m_old - m_new)
        p = jnp.exp(s - m_new)
        l = alpha * l + jnp.sum(p, axis=1, keepdims=True)
        p16 = jnp.concatenate([p, jnp.zeros_like(p)], axis=0).astype(BF16)
        acc = alpha * acc + _dot(p16, cb)[:N_HEADS]
        return m_new, l, acc

    def pair(i, carry):
        start(b, 2 * i + 1, 1)
        wait(0)
        carry = chunk_step(0, carry)
        last = i == n_chunks // 2 - 1

        @pl.when(jnp.logical_not(last))
        def _():
            start(b, 2 * i + 2, 0)

        @pl.when(jnp.logical_and(last, b + 1 < nb))
        def _():
            start(b + 1, 0, 0)

        wait(1)
        return chunk_step(1, carry)

    _, l, acc = lax.fori_loop(0, n_chunks // 2, pair, init)
    o_ref[0] = acc / l


def _mla_decode(layer, page_table, qabs, qrope, c_new, pe_new, inv_new, cache_ckv, cache_kpe_t, cache_kinv_t,
                n_pg):
    nb, n_pages = page_table.shape
    assert n_pages % (2 * n_pg) == 0
    hr = 2 * N_HEADS
    anyspec = pl.BlockSpec(memory_space=pl.ANY)
    return pl.pallas_call(
        functools.partial(_mla_decode_kernel, layer=layer, n_pg=n_pg, n_chunks=n_pages // n_pg),
        grid_spec=pltpu.PrefetchScalarGridSpec(
            num_scalar_prefetch=1,
            grid=(nb,),
            in_specs=[pl.BlockSpec((1, hr, KV_LORA), lambda b, pt: (b, 0, 0)),
                      pl.BlockSpec((1, hr, ROPE), lambda b, pt: (b, 0, 0)),
                      pl.BlockSpec((1, 1, KV_LORA), lambda b, pt: (b, 0, 0)),
                      pl.BlockSpec((1, 1, ROPE), lambda b, pt: (b, 0, 0)),
                      pl.BlockSpec((1, N_HEADS, 1), lambda b, pt: (b, 0, 0)),
                      anyspec, anyspec, anyspec],
            out_specs=pl.BlockSpec((1, N_HEADS, KV_LORA), lambda b, pt: (b, 0, 0)),
            scratch_shapes=[pltpu.VMEM((2, n_pg, PAGE, KV_LORA), F32),
                            pltpu.VMEM((2, n_pg, ROPE, PAGE), F32),
                            pltpu.VMEM((2, n_pg, N_HEADS, PAGE), F32),
                            pltpu.SemaphoreType.DMA((2, 3))]),
        out_shape=jax.ShapeDtypeStruct((nb, N_HEADS, KV_LORA), F32),
        compiler_params=_params(("arbitrary",)),
        name="mla_decode",
    )(page_table, qabs, qrope, c_new, pe_new, inv_new, cache_ckv, cache_kpe_t, cache_kinv_t)


def _sb_page(q_ref, kt_ref, vt_ref, tail, accs):
    head = lax.broadcasted_iota(I32, (N_HEADS, PAGE), 0)
    lane = lax.broadcasted_iota(I32, (N_HEADS, PAGE), 1)
    z = jnp.zeros((N_HEADS, PAGE), F32)
    for h in range(N_HEADS):
        zh = jnp.sum(kt_ref[h] * q_ref[h], axis=0, keepdims=True)
        z = jnp.where(head == h, zh, z)
    sp = _softplus(z)
    lf = -sp
    suf = lf
    shift = 1
    while shift < PAGE:
        moved = pltpu.roll(suf, PAGE - shift, 1)
        suf = suf + jnp.where(lane < PAGE - shift, moved, 0.0)
        shift *= 2
    a = jnp.exp((z - sp) + (suf - lf) + tail)
    new_accs = []
    for h in range(N_HEADS):
        new_accs.append(accs[h] + jnp.sum(vt_ref[h] * a[h:h + 1, :], axis=1, keepdims=True))
    return tail + suf[:, 0:1], new_accs


def _sb_decode_kernel(pt_ref, q_ref, k0_ref, v0_ref, k1_ref, v1_ref, kc_ref, vc_ref, o_ref,
                      kbuf, vbuf, sem, *, layer, n_pages, n_pre):
    b = pl.program_id(0)
    q = q_ref.at[0]
    tail = jnp.zeros((N_HEADS, 1), F32)
    accs = [jnp.zeros((SB_DIM, 1), F32) for _ in range(N_HEADS)]
    tail, accs = _sb_page(q, k0_ref.at[0, 0], v0_ref.at[0, 0], tail, accs)
    if n_pre > 1:
        tail, accs = _sb_page(q, k1_ref.at[0, 0], v1_ref.at[0, 0], tail, accs)

    def cond(c):
        return jnp.logical_and(c[0] >= 0, c[1] == 0)

    def body(c):
        j, _, tail, accs = c
        page = pt_ref[b, j]
        ck = pltpu.make_async_copy(kc_ref.at[layer, page], kbuf, sem.at[0])
        cv = pltpu.make_async_copy(vc_ref.at[layer, page], vbuf, sem.at[1])
        ck.start()
        cv.start()
        ck.wait()
        cv.wait()
        tail, accs = _sb_page(q, kbuf, vbuf, tail, list(accs))
        return j - 1, (jnp.max(tail) < SB_DONE).astype(I32), tail, tuple(accs)

    done0 = (jnp.max(tail) < SB_DONE).astype(I32)
    accs = lax.while_loop(cond, body, (jnp.int32(n_pages - 1 - n_pre), done0, tail, tuple(accs)))[3]
    for h in range(N_HEADS):
        o_ref[0, h] = accs[h]


def _sb_decode(layer, page_table, q, cache_kt, cache_vt):
    nb, n_pages = page_table.shape
    n_pre = 2 if n_pages >= 2 else 1
    page_shape = (N_HEADS, SB_DIM, PAGE)

    def pg(back):
        return pl.BlockSpec((1, 1) + page_shape, lambda b, pt: (layer, pt[b, n_pages - 1 - back], 0, 0, 0))

    qspec = pl.BlockSpec((1, N_HEADS, SB_DIM, 1), lambda b, pt: (b, 0, 0, 0))
    return pl.pallas_call(
        functools.partial(_sb_decode_kernel, layer=layer, n_pages=n_pages, n_pre=n_pre),
        grid_spec=pltpu.PrefetchScalarGridSpec(
            num_scalar_prefetch=1,
            grid=(nb,),
            in_specs=[qspec, pg(0), pg(0), pg(n_pre - 1), pg(n_pre - 1),
                      pl.BlockSpec(memory_space=pl.ANY), pl.BlockSpec(memory_space=pl.ANY)],
            out_specs=qspec,
            scratch_shapes=[pltpu.VMEM(page_shape, F32), pltpu.VMEM(page_shape, F32),
                            pltpu.SemaphoreType.DMA((2,))]),
        out_shape=jax.ShapeDtypeStruct((nb, N_HEADS, SB_DIM, 1), F32),
        compiler_params=_params(("arbitrary",)),
        name="sb_decode",
    )(page_table, q, cache_kt, cache_vt, cache_kt, cache_vt, cache_kt, cache_vt)


def _uv_kernel(o_ref, w_ref, out_ref):
    for hd in range(N_HEADS):
        out_ref[:, hd * HEAD_PAD:(hd + 1) * HEAD_PAD] = _dot(
            o_ref[:, hd * KV_LORA:(hd + 1) * KV_LORA].astype(BF16), w_ref[:, hd * HEAD_PAD:(hd + 1) * HEAD_PAD]
        ).astype(BF16)


def _value_up(o_lat, w_uv_p):
    t = o_lat.shape[0]
    return pl.pallas_call(
        _uv_kernel,
        out_shape=jax.ShapeDtypeStruct((t, N_HEADS * HEAD_PAD), BF16),
        compiler_params=pltpu.CompilerParams(vmem_limit_bytes=VMEM_LIMIT),
        name="value_up",
    )(o_lat, w_uv_p)


def _post_kernel(x_ref, om_ref, os_ref, ga_ref, shm_ref, scm_ref, gm_ref, gmoe_ref, wom_ref, wos_ref,
                 wgu_ref, wd_ref, base_ref, h2_ref):
    attn = _dot(om_ref[...], wom_ref[...]) + _dot(os_ref[...], wos_ref[...])
    xm = x_ref[...] + ga_ref[0] * attn
    h2 = _rms(xm, gmoe_ref[...]) * (1.0 + scm_ref[0]) + shm_ref[0]
    h2_ref[...] = h2
    gu = _dot(h2.astype(BF16), wgu_ref[...])
    g = gu[:, :D_SHARED]
    act = (g * _sigmoid(g)) * gu[:, D_SHARED:]
    ysh = _dot(act.astype(BF16), wd_ref[...])
    base_ref[...] = xm + gm_ref[0] * ysh


def _post_attention(x, o_mla, o_sb, mods, wts, *, tm, per_token_mod):
    t = x.shape[0]
    n_tiles = t // tm
    g_a, sh_m, sc_m, g_m = mods
    if per_token_mod:
        mod_spec = pl.BlockSpec((1, tm, D_MODEL), lambda i: (i, 0, 0))
    else:
        tiles_per_seq = n_tiles // g_a.shape[0]
        mod_spec = pl.BlockSpec((1, 1, D_MODEL), lambda i: (i // tiles_per_seq, 0, 0))

    def full(a):
        return pl.BlockSpec(a.shape, lambda i: (0,) * a.ndim)

    def rows(width):
        return pl.BlockSpec((tm, width), lambda i: (i, 0))

    buf = jax.ShapeDtypeStruct((t, D_MODEL), F32)
    return pl.pallas_call(
        _post_kernel,
        grid=(n_tiles,),
        in_specs=[rows(D_MODEL), rows(N_HEADS * HEAD_PAD), rows(SB_WIDTH), mod_spec, mod_spec, mod_spec,
                  mod_spec, full(wts["g_moe"]), full(wts["w_o_mla"]), full(wts["w_o_sb"]),
                  full(wts["w_sh_gu"]), full(wts["w_sh_down"])],
        out_specs=[rows(D_MODEL), rows(D_MODEL)],
        out_shape=[buf, buf],
        compiler_params=_params(("arbitrary",)),
        name="post_attention",
    )(x, o_mla, o_sb, g_a, sh_m, sc_m, g_m, wts["g_moe"], wts["w_o_mla"], wts["w_o_sb"],
      wts["w_sh_gu"], wts["w_sh_down"])


def _router_kernel(hp_ref, hs_ref, whi_ref, wlo_ref, b_ref, idx_ref, wgt_ref, rank_ref, cnt_ref, *,
                   n_prompt_tiles):
    i = pl.program_id(0)

    @pl.when(i == 0)
    def _():
        cnt_ref[...] = jnp.zeros_like(cnt_ref)

    h = jnp.where(i < n_prompt_tiles, hp_ref[...], hs_ref[...])
    tm = h.shape[0]
    hhi, hlo = _split_bf16(h)
    whi = whi_ref[...]
    logits = _dot_nt(whi, hhi) + _dot_nt(whi, hlo) + _dot_nt(wlo_ref[...], hhi)
    scores = _sigmoid(logits)
    sel = scores + b_ref[...]
    neg = -jnp.inf

    blocks = [sel[g * GROUP_SIZE:(g + 1) * GROUP_SIZE, :] for g in range(N_GROUPS)]
    gs = []
    for blk in blocks:
        m1 = jnp.max(blk, axis=0, keepdims=True)
        is_max = blk == m1
        n_max = jnp.sum(is_max.astype(F32), axis=0, keepdims=True)
        m2 = jnp.max(jnp.where(is_max, neg, blk), axis=0, keepdims=True)
        gs.append(m1 + jnp.where(n_max >= 2.0, m1, m2))
    kept = []
    for g in range(N_GROUPS):
        ahead = jnp.zeros_like(gs[g])
        for g2 in range(N_GROUPS):
            if g2 == g:
                continue
            better = gs[g2] > gs[g]
            if g2 < g:
                better = jnp.logical_or(better, gs[g2] == gs[g])
            ahead = ahead + better.astype(F32)
        kept.append(jnp.where(ahead < float(TOPK_GROUPS), blocks[g], neg))
    cur = jnp.concatenate(kept, axis=0)

    rowf = lax.broadcasted_iota(I32, cur.shape, 0).astype(F32)
    chosen = jnp.zeros_like(cur)
    idxs, ws = [], []
    for _ in range(TOP_K):
        m = jnp.max(cur, axis=0, keepdims=True)
        ik = jnp.min(jnp.where(cur == m, rowf, float(N_EXPERTS)), axis=0, keepdims=True)
        hit = rowf == ik
        ws.append(jnp.sum(jnp.where(hit, scores, 0.0), axis=0, keepdims=True))
        cur = jnp.where(hit, neg, cur)
        chosen = chosen + hit.astype(F32)
        idxs.append(ik)
    wsum = ws[0]
    for w in ws[1:]:
        wsum = wsum + w
    ws = [w / wsum * ROUTE_SCALE for w in ws]

    r2 = lax.broadcasted_iota(I32, (tm, tm), 0)
    c2 = lax.broadcasted_iota(I32, (tm, tm), 1)
    left = (r2 < c2).astype(BF16)
    pos = _dot(chosen.astype(BF16), left) + cnt_ref[...]
    ranks = [jnp.sum(jnp.where(rowf == ik, pos, 0.0), axis=0, keepdims=True) for ik in idxs]
    cnt_ref[...] = cnt_ref[...] + jnp.sum(chosen, axis=1, keepdims=True)

    idx_ref[...] = jnp.concatenate(idxs, axis=0).astype(I32)
    wgt_ref[...] = jnp.concatenate(ws, axis=0)
    rank_ref[...] = jnp.concatenate(ranks, axis=0).astype(I32)


def _route(h2_p, h2_s, whi, wlo, b_col, tm):
    n_p = h2_p.shape[0] // tm
    t = h2_p.shape[0] + h2_s.shape[0]
    sm = pl.BlockSpec((TOP_K, tm), lambda i: (0, i))
    return pl.pallas_call(
        functools.partial(_router_kernel, n_prompt_tiles=n_p),
        grid=(t // tm,),
        in_specs=[pl.BlockSpec((tm, D_MODEL), lambda i: (jnp.minimum(i, n_p - 1), 0)),
                  pl.BlockSpec((tm, D_MODEL), lambda i: (jnp.maximum(i - n_p, 0), 0)),
                  pl.BlockSpec(whi.shape, lambda i: (0, 0)),
                  pl.BlockSpec(wlo.shape, lambda i: (0, 0)),
                  pl.BlockSpec(b_col.shape, lambda i: (0, 0))],
        out_specs=[sm, sm, sm, pl.BlockSpec((N_EXPERTS, 1), lambda i: (0, 0))],
        out_shape=[jax.ShapeDtypeStruct((TOP_K, t), I32), jax.ShapeDtypeStruct((TOP_K, t), F32),
                   jax.ShapeDtypeStruct((TOP_K, t), I32), jax.ShapeDtypeStruct((N_EXPERTS, 1), F32)],
        compiler_params=_params(("arbitrary",)),
        name="router",
    )(h2_p, h2_s, whi, wlo, b_col)


def _row(ref, r):
    return ref.at[pl.ds(r, 1), :]


def _dispatch_kernel(dest_ref, hp_ref, hs_ref, xs_ref, sem, *, tm, n_prompt_tiles):
    i = pl.program_id(0)

    def scatter_rows(src_ref):
        def issue(t, carry):
            for k in range(TOP_K):
                pltpu.make_async_copy(_row(src_ref, t), _row(xs_ref, dest_ref[k, t]), sem).start(priority=k % 2)
            return carry

        lax.fori_loop(0, tm, issue, 0)

        def drain(t, carry):
            for _ in range(TOP_K):
                pltpu.make_async_copy(_row(src_ref, 0), _row(xs_ref, 0), sem).wait()
            return carry

        lax.fori_loop(0, tm, drain, 0)

    @pl.when(i < n_prompt_tiles)
    def _():
        scatter_rows(hp_ref)

    @pl.when(i >= n_prompt_tiles)
    def _():
        scatter_rows(hs_ref)


def _dispatch(dest, h2_p, h2_s, n_rows, tm):
    n_p = h2_p.shape[0] // tm
    t = h2_p.shape[0] + h2_s.shape[0]
    return pl.pallas_call(
        functools.partial(_dispatch_kernel, tm=tm, n_prompt_tiles=n_p),
        grid=(t // tm,),
        in_specs=[pl.BlockSpec((TOP_K, tm), lambda i: (0, i), memory_space=pltpu.SMEM),
                  pl.BlockSpec((tm, D_MODEL), lambda i: (jnp.minimum(i, n_p - 1), 0)),
                  pl.BlockSpec((tm, D_MODEL), lambda i: (jnp.maximum(i - n_p, 0), 0))],
        out_specs=pl.BlockSpec(memory_space=pl.ANY),
        scratch_shapes=[pltpu.SemaphoreType.DMA(())],
        out_shape=jax.ShapeDtypeStruct((n_rows, D_MODEL), F32),
        compiler_params=_params(("arbitrary",)),
        name="dispatch",
    )(dest, h2_p, h2_s)


def _expert_kernel(be_ref, nb_ref, xs_ref, wg_ref, wu_ref, wd_ref, ys_ref, wgu_s, wd_s):
    i = pl.program_id(0)

    @pl.when(i < nb_ref[0])
    def _():
        prev = be_ref[jnp.maximum(i - 1, 0)]

        @pl.when(jnp.logical_or(i == 0, be_ref[i] != prev))
        def _():
            wgu_s[:, :D_EXPERT] = wg_ref[0].astype(BF16)
            wgu_s[:, D_EXPERT:] = wu_ref[0].astype(BF16)
            wd_s[...] = wd_ref[0].astype(BF16)

        gu = _dot(xs_ref[...].astype(BF16), wgu_s[...])
        g = gu[:, :D_EXPERT]
        act = (g * _sigmoid(g)) * gu[:, D_EXPERT:]
        ys_ref[...] = _dot(act.astype(BF16), wd_s[...])


def _experts(layer, block_e, n_used, xs, w_gate, w_up, w_down):
    n_rows = xs.shape[0]
    n_blocks = n_rows // EXPERT_BLOCK
    w_gate, w_up, w_down = (w.reshape((-1,) + w.shape[2:]) for w in (w_gate, w_up, w_down))

    def row_map(i, be, nb):
        return (jnp.minimum(i, nb[0] - 1), 0)

    def w_map(i, be, nb):
        return (layer * N_EXPERTS + be[jnp.minimum(i, nb[0] - 1)], 0, 0)

    return pl.pallas_call(
        _expert_kernel,
        grid_spec=pltpu.PrefetchScalarGridSpec(
            num_scalar_prefetch=2,
            grid=(n_blocks,),
            in_specs=[pl.BlockSpec((EXPERT_BLOCK, D_MODEL), row_map),
                      pl.BlockSpec((1, D_MODEL, D_EXPERT), w_map),
                      pl.BlockSpec((1, D_MODEL, D_EXPERT), w_map),
                      pl.BlockSpec((1, D_EXPERT, D_MODEL), w_map)],
            out_specs=pl.BlockSpec((EXPERT_BLOCK, D_MODEL), row_map),
            scratch_shapes=[pltpu.VMEM((D_MODEL, 2 * D_EXPERT), BF16), pltpu.VMEM((D_EXPERT, D_MODEL), BF16)]),
        out_shape=jax.ShapeDtypeStruct(xs.shape, F32),
        compiler_params=_params(("arbitrary",)),
        name="experts",
    )(block_e, n_used, xs, w_gate, w_up, w_down)


def _combine_kernel(dest_ref, wgt_ref, ys_ref, basep_ref, bases_ref, gmp_ref, gms_ref,
                    outp_ref, outs_ref, buf, sem, *, tm, n_prompt_tiles):
    i = pl.program_id(0)

    def issue(t, carry):
        for k in range(TOP_K):
            pltpu.make_async_copy(_row(ys_ref, dest_ref[k, t]), _row(buf.at[k], t), sem).start(priority=k % 2)
        return carry

    lax.fori_loop(0, tm, issue, 0)

    def drain(t, carry):
        for k in range(TOP_K):
            pltpu.make_async_copy(_row(ys_ref, 0), _row(buf.at[k], 0), sem).wait()
        return carry

    lax.fori_loop(0, tm, drain, 0)

    w = wgt_ref[...]
    mix = w[:, 0:1] * buf[0]
    for k in range(1, TOP_K):
        mix = mix + w[:, k:k + 1] * buf[k]

    @pl.when(i < n_prompt_tiles)
    def _():
        outp_ref[...] = basep_ref[...] + gmp_ref[0] * mix

    @pl.when(i >= n_prompt_tiles)
    def _():
        outs_ref[...] = bases_ref[...] + gms_ref[...] * mix


def _combine(dest, wgt_t, ys, base_p, base_s, gm_prompt, gm_sample, tm, seq):
    t_all = base_p.shape[0] + base_s.shape[0]
    n_batch = gm_prompt.shape[0]
    n_prompt_tiles = n_batch * seq // tm
    tiles_per_seq = seq // tm
    tile = (tm, D_MODEL)
    prompt_tile = pl.BlockSpec(tile, lambda i: (jnp.minimum(i, n_prompt_tiles - 1), 0))
    sample_tile = pl.BlockSpec(tile, lambda i: (jnp.maximum(i - n_prompt_tiles, 0), 0))
    return pl.pallas_call(
        functools.partial(_combine_kernel, tm=tm, n_prompt_tiles=n_prompt_tiles),
        grid=(t_all // tm,),
        in_specs=[pl.BlockSpec((TOP_K, tm), lambda i: (0, i), memory_space=pltpu.SMEM),
                  pl.BlockSpec((tm, TOP_K), lambda i: (i, 0)),
                  pl.BlockSpec(memory_space=pl.ANY), prompt_tile, sample_tile,
                  pl.BlockSpec((1, 1, D_MODEL), lambda i: (jnp.minimum(i // tiles_per_seq, n_batch - 1), 0, 0)),
                  sample_tile],
        out_specs=[prompt_tile, sample_tile],
        scratch_shapes=[pltpu.VMEM((TOP_K, tm, D_MODEL), F32), pltpu.SemaphoreType.DMA(())],
        out_shape=[jax.ShapeDtypeStruct(base_p.shape, F32), jax.ShapeDtypeStruct(base_s.shape, F32)],
        compiler_params=_params(("arbitrary",)),
        name="combine",
    )(dest, wgt_t, ys, base_p, base_s, gm_prompt, gm_sample)


def _pad_heads(w, width):
    k = w.shape[0]
    w = w.reshape(k, N_HEADS, width)
    return jnp.pad(w, ((0, 0), (0, 0), (0, HEAD_PAD - width))).reshape(k, N_HEADS * HEAD_PAD)


def _rope_tables(pos):
    half = ROPE // 2
    inv_freq = ROPE_THETA ** (-jnp.arange(half, dtype=F32) * (2.0 / ROPE))
    ang = pos.astype(F32)[:, None] * inv_freq[None, :]
    cos, sin = jnp.cos(ang), jnp.sin(ang)
    n = pos.shape[0]
    ones = jnp.ones((n, NOPE), F32)
    zeros = jnp.zeros((n, HEAD_PAD - QK), F32)
    cos_t = jnp.concatenate([ones, cos, cos, zeros], axis=1)
    sin_t = jnp.concatenate([jnp.zeros((n, NOPE), F32), -sin, sin, zeros], axis=1)
    return cos_t, sin_t


def _layer_weights(l, w_ada, b_ada, g_attn, g_moe, w_in, g_q_lat, w_uq, g_kv_lat, w_uk, w_uv, g_qk_q, g_qk_k,
                   w_o, w_router, b_router, w_sh_gate, w_sh_up, w_sh_down):
    wi = w_in[l]
    o1 = Q_LORA + KV_LORA
    o2 = o1 + ROPE
    kr_block = jnp.zeros((D_MODEL, HEAD_PAD), F32).at[:, NOPE:QK].set(wi[:, o1:o2])
    w_in_p = jnp.concatenate([wi[:, :o1], wi[:, o2:], kr_block], axis=1).astype(BF16)
    gq = jnp.concatenate([g_qk_q[l][:NOPE] * g_qk_k[l][:NOPE], g_qk_q[l][NOPE:],
                          jnp.zeros((HEAD_PAD - QK,), F32)]).reshape(1, HEAD_PAD)
    gkr = jnp.concatenate([jnp.zeros((NOPE,), F32), g_qk_k[l][NOPE:],
                           jnp.zeros((HEAD_PAD - QK,), F32)]).reshape(1, HEAD_PAD)
    wo = w_o[l]
    n_mla = N_HEADS * NOPE
    w_o_mla = jnp.pad(wo[:n_mla].reshape(N_HEADS, NOPE, D_MODEL),
                      ((0, 0), (0, HEAD_PAD - NOPE), (0, 0))).reshape(N_HEADS * HEAD_PAD, D_MODEL)
    wr_t = w_router[l].T
    whi = wr_t.astype(BF16)
    wlo = (wr_t - whi.astype(F32)).astype(BF16)
    return dict(
        w_ada=w_ada[l], b_ada=b_ada[l],
        g_attn=g_attn[l].reshape(1, -1), g_moe=g_moe[l].reshape(1, -1),
        g_q_lat=g_q_lat[l].reshape(1, -1), g_kv_lat=g_kv_lat[l].reshape(1, -1), gq=gq, gkr=gkr,
        w_in=w_in_p, w_uq=_pad_heads(w_uq[l], QK).astype(BF16),
        w_uk=_pad_heads(w_uk[l], NOPE).astype(BF16), w_uv=_pad_heads(w_uv[l], NOPE).astype(BF16),
        w_o_mla=w_o_mla.astype(BF16), w_o_sb=wo[n_mla:].astype(BF16),
        w_r_hi=whi, w_r_lo=wlo, b_router=b_router[l].reshape(-1, 1),
        w_sh_gu=jnp.concatenate([w_sh_gate[l], w_sh_up[l]], axis=1).astype(BF16),
        w_sh_down=w_sh_down[l].astype(BF16),
    )


def _tile(n, pref):
    t = min(n, pref)
    while n % t:
        t //= 2
    return t


def _layer(l, xp, xs_, cache_mla_ckv, cache_mla_kpe, cache_mla_kinv, cache_sb_k, cache_sb_v, page_table,
           c_prompt, c_sample, wts, w_exp_gate, w_exp_up, w_exp_down):
    batch, seq, _ = xp.shape
    nb = xs_.shape[0]
    n_pages = page_table.shape[1]
    past = n_pages * PAGE
    t_p = batch * seq
    tok_tile = LANES
    assert xs_.shape[1] == 1 and nb % tok_tile == 0 and seq % tok_tile == 0
    t_s = nb
    tm_s = tok_tile
    t_all = t_p + t_s

    mods = _adaln(jnp.concatenate([c_prompt, c_sample], axis=0), wts["w_ada"], wts["b_ada"])
    mods_p = [m.reshape(batch, 1, D_MODEL) for m in jnp.split(mods[:batch], 6, axis=-1)]
    mods_s = [m.reshape(t_s // tm_s, tm_s, D_MODEL) for m in jnp.split(mods[batch:], 6, axis=-1)]

    tm_p = _tile(seq, 512)
    cos_p, sin_p = _rope_tables(jnp.arange(seq))
    (q, k, v, sbq, sbkb, sbvb, ckv_p, kpe_p, kinv_p, sbk_p, sbv_p) = _pre_project(
        xp.reshape(t_p, D_MODEL), mods_p[0], mods_p[1], cos_p, sin_p, wts,
        tm=tm_p, per_token_mod=False, pos_tiles=seq // tm_p, with_qabs=False)
    o_mla_p = _mla_prompt(q, k, v, batch, seq, _tile(seq, 512), 2)
    o_sb_p = _sb_prompt(sbq, sbkb, sbvb, batch, seq, _tile(seq, 256))

    cos_s, sin_s = _rope_tables(jnp.full((tm_s,), past, I32))
    (q_s, _, _, sbq_s, _, _, ckv_s, kpe_s, kinv_s, sbk_s, sbv_s, qabs_s) = _pre_project(
        xs_.reshape(t_s, D_MODEL), mods_s[0], mods_s[1], cos_s, sin_s, wts,
        tm=tm_s, per_token_mod=True, pos_tiles=1, with_qabs=True)
    head_rows = N_HEADS
    qabs3 = jnp.pad(qabs_s.reshape(t_s, N_HEADS, KV_LORA), ((0, 0), (0, head_rows), (0, 0)))
    qrope3 = jnp.pad(q_s.reshape(t_s, N_HEADS, HEAD_PAD)[:, :, NOPE:QK], ((0, 0), (0, head_rows), (0, 0)))
    n_pg = min(8, n_pages // 2)
    o_lat = _mla_decode(l, page_table, qabs3, qrope3, ckv_s.reshape(t_s, 1, KV_LORA),
                        kpe_s[:, NOPE:QK].reshape(t_s, 1, ROPE), kinv_s[:, :N_HEADS].reshape(t_s, N_HEADS, 1),
                        cache_mla_ckv, jnp.swapaxes(cache_mla_kpe, 2, 3), jnp.swapaxes(cache_mla_kinv, 2, 3),
                        n_pg)
    o_mla_s = _value_up(o_lat.reshape(t_s, N_HEADS * KV_LORA), wts["w_uv"])
    o_sb_s = _sb_decode(l, page_table, sbq_s.astype(F32).reshape(t_s, N_HEADS, SB_DIM, 1),
                        cache_sb_k.transpose(0, 1, 3, 4, 2), cache_sb_v.transpose(0, 1, 3, 4, 2))
    o_sb_s = o_sb_s.reshape(t_s, SB_WIDTH).astype(BF16)

    base_p, h2_p = _post_attention(xp.reshape(t_p, D_MODEL), o_mla_p, o_sb_p, mods_p[2:], wts,
                                   tm=tm_p, per_token_mod=False)
    base_s, h2_s = _post_attention(xs_.reshape(t_s, D_MODEL), o_mla_s, o_sb_s, mods_s[2:], wts,
                                   tm=tm_s, per_token_mod=True)

    tm_r = tok_tile
    idx, wgt, rank, counts = _route(h2_p, h2_s, wts["w_r_hi"], wts["w_r_lo"], wts["b_router"], tm_r)
    counts = counts.reshape(N_EXPERTS).astype(I32)
    padded = (counts + EXPERT_BLOCK - 1) // EXPERT_BLOCK * EXPERT_BLOCK
    pend = jnp.cumsum(padded)
    pstart = (pend - padded).astype(I32)
    n_assign = t_all * TOP_K
    n_rows = -(-n_assign // EXPERT_BLOCK) * EXPERT_BLOCK + N_EXPERTS * EXPERT_BLOCK
    n_blocks = n_rows // EXPERT_BLOCK
    block_start = jnp.arange(n_blocks, dtype=I32) * EXPERT_BLOCK
    block_e = jnp.minimum(jnp.sum((pend[None, :] <= block_start[:, None]).astype(I32), axis=1), N_EXPERTS - 1)
    n_used = (pend[-1:] // EXPERT_BLOCK).astype(I32)
    dest = pstart[idx] + rank
    xs_rows = _dispatch(dest, h2_p, h2_s, n_rows, tm_r)
    ys_rows = _experts(l, block_e, n_used, xs_rows, w_exp_gate, w_exp_up, w_exp_down)
    y_p, y_s = _combine(dest, wgt.T, ys_rows, base_p, base_s, mods_p[5],
                        mods_s[5].reshape(t_s, D_MODEL), tm_r, seq)
    y_p = y_p.reshape(batch, seq, D_MODEL)
    y_s = y_s.reshape(nb, 1, D_MODEL)

    new_p = (ckv_p.reshape(batch, seq, KV_LORA), kpe_p[:, NOPE:QK].reshape(batch, seq, ROPE),
             kinv_p[:, :N_HEADS].reshape(batch, seq, N_HEADS),
             sbk_p.reshape(batch, seq, N_HEADS, SB_DIM), sbv_p.reshape(batch, seq, N_HEADS, SB_DIM))
    new_s = (ckv_s.reshape(nb, 1, KV_LORA), kpe_s[:, NOPE:QK].reshape(nb, 1, ROPE),
             kinv_s[:, :N_HEADS].reshape(nb, 1, N_HEADS),
             sbk_s.reshape(nb, 1, N_HEADS, SB_DIM), sbv_s.reshape(nb, 1, N_HEADS, SB_DIM))
    return y_p, y_s, new_p, new_s


def kernel(x_prompt, x_sample, cache_mla_ckv, cache_mla_kpe, cache_mla_kinv, cache_sb_k, cache_sb_v, page_table, c_prompt, c_sample, w_ada, b_ada, g_attn, g_moe, w_in, g_q_lat, w_uq, g_kv_lat, w_uk, w_uv, g_qk_q, g_qk_k, w_o, w_router, b_router, w_exp_gate, w_exp_up, w_exp_down, w_sh_gate, w_sh_up, w_sh_down):
    depth = w_in.shape[0]
    y_p, y_s = x_prompt, x_sample
    p_rows = [[] for _ in range(5)]
    s_rows = [[] for _ in range(5)]
    for l in range(depth):
        wts = _layer_weights(l, w_ada, b_ada, g_attn, g_moe, w_in, g_q_lat, w_uq, g_kv_lat, w_uk, w_uv,
                             g_qk_q, g_qk_k, w_o, w_router, b_router, w_sh_gate, w_sh_up, w_sh_down)
        y_p, y_s, new_p, new_s = _layer(l, y_p, y_s, cache_mla_ckv, cache_mla_kpe, cache_mla_kinv,
                                        cache_sb_k, cache_sb_v, page_table, c_prompt, c_sample, wts,
                                        w_exp_gate, w_exp_up, w_exp_down)
        for lst, a in zip(p_rows, new_p):
            lst.append(a)
        for lst, a in zip(s_rows, new_s):
            lst.append(a)
    outs_p = [jnp.stack(r) for r in p_rows]
    outs_s = [jnp.stack(r) for r in s_rows]
    return (y_p, y_s, *outs_p, *outs_s)
```

```python
import functools

import jax
import jax.numpy as jnp
from jax import lax
from jax.experimental import pallas as pl
from jax.experimental.pallas import tpu as pltpu

F32 = jnp.float32
BF16 = jnp.bfloat16
I32 = jnp.int32

D_MODEL = 1024
PAGE = 128
N_HEADS = 8
NOPE = 64
ROPE = 32
QK = NOPE + ROPE
HEAD_PAD = 128
Q_LORA = 768
KV_LORA = 256
SB_DIM = 64
SB_WIDTH = N_HEADS * SB_DIM
ROPE_THETA = 10000.0
MLA_SCALE = QK ** -0.5
SB_SCALE = SB_DIM ** -0.5
LOG2E = 1.4426950408889634
N_EXPERTS = 256
TOP_K = 8
N_GROUPS = 8
GROUP_SIZE = N_EXPERTS // N_GROUPS
TOPK_GROUPS = 4
D_EXPERT = 256
D_SHARED = 256
ROUTE_SCALE = 2.5
EPS = 1e-6
LANES = 128
SUBLANES = 8
ROW_CHUNKS = D_MODEL // LANES
SB_DONE = -110.0
EXPERT_BLOCK = 256
DECODE_DEPTH = 2
DECODE_SLOTS = DECODE_DEPTH + 1
VMEM_LIMIT = 56 * 1024 * 1024

NT = (((1,), (1,)), ((), ()))


def _params(sem):
    return pltpu.CompilerParams(dimension_semantics=sem, vmem_limit_bytes=VMEM_LIMIT)


def _dot(a, b):
    return jnp.dot(a, b, preferred_element_type=F32)


def _dot_nt(a, b):
    return lax.dot_general(a, b, NT, preferred_element_type=F32)


def _split_bf16(x):
    hi = x.astype(BF16)
    lo = (x - hi.astype(F32)).astype(BF16)
    return hi, lo


def _sigmoid(x):
    return 1.0 / (1.0 + jnp.exp(-x))


def _softplus(x):
    return jnp.maximum(x, 0.0) + jnp.log(1.0 + jnp.exp(-jnp.abs(x)))


def _rms(x, g):
    return x * lax.rsqrt(jnp.mean(x * x, axis=-1, keepdims=True) + EPS) * g


def _rope128(x, cos_t, sin_t):
    lane = lax.broadcasted_iota(I32, x.shape, 1)
    up = pltpu.roll(x, LANES - ROPE // 2, 1)
    dn = pltpu.roll(x, ROPE // 2, 1)
    rot = jnp.where(lane < NOPE + ROPE // 2, up, dn)
    return x * cos_t + rot * sin_t


def _ada_kernel(c_ref, w_ref, b_ref, o_ref):
    c = c_ref[...]
    s = c * _sigmoid(c)
    o_ref[...] = _dot(s.astype(BF16), w_ref[...].astype(BF16)) + b_ref[...]


def _adaln(c, w_ada, b_ada):
    rows = c.shape[0]
    n = w_ada.shape[1]
    return pl.pallas_call(
        _ada_kernel,
        grid=(n // D_MODEL,),
        in_specs=[pl.BlockSpec((rows, D_MODEL), lambda j: (0, 0)),
                  pl.BlockSpec((D_MODEL, D_MODEL), lambda j: (0, j)),
                  pl.BlockSpec((1, D_MODEL), lambda j: (0, j))],
        out_specs=pl.BlockSpec((rows, D_MODEL), lambda j: (0, j)),
        out_shape=jax.ShapeDtypeStruct((rows, n), F32),
        compiler_params=_params(("arbitrary",)),
        name="adaln",
    )(c, w_ada, b_ada.reshape(1, n))


def _pre_kernel(x_ref, sh_ref, sc_ref, cos_ref, sin_ref, gattn_ref, gql_ref, gkv_ref, gq_ref, gkr_ref,
                win_ref, wuq_ref, wuk_ref, wuv_ref,
                q_ref, k_ref, v_ref, sbq_ref, sbkb_ref, sbvb_ref, ckv_ref, kpe_ref, kinv_ref,
                sbk_ref, sbv_ref, *maybe_qabs_ref):
    tm = x_ref.shape[0]
    n_split = 2 if tm >= 2 * LANES else 1
    for part in range(n_split):
        rs = slice(part * (tm // n_split), (part + 1) * (tm // n_split))
        _pre_rows(rs, x_ref, sh_ref, sc_ref, cos_ref, sin_ref, gattn_ref, gql_ref, gkv_ref, gq_ref, gkr_ref,
                  win_ref, wuq_ref, wuk_ref, wuv_ref,
                  q_ref, k_ref, v_ref, sbq_ref, sbkb_ref, sbvb_ref, ckv_ref, kpe_ref, kinv_ref,
                  sbk_ref, sbv_ref, maybe_qabs_ref)


def _pre_rows(rs, x_ref, sh_ref, sc_ref, cos_ref, sin_ref, gattn_ref, gql_ref, gkv_ref, gq_ref, gkr_ref,
              win_ref, wuq_ref, wuk_ref, wuv_ref,
              q_ref, k_ref, v_ref, sbq_ref, sbkb_ref, sbvb_ref, ckv_ref, kpe_ref, kinv_ref,
              sbk_ref, sbv_ref, maybe_qabs_ref):
    def mod(ref):
        return ref[0] if ref.shape[1] == 1 else ref[0, rs, :]

    x = x_ref[rs, :]
    h = _rms(x, gattn_ref[...]) * (1.0 + mod(sc_ref)) + mod(sh_ref)
    proj = _dot(h.astype(BF16), win_ref[...])
    o_kv = Q_LORA
    o_sbq = o_kv + KV_LORA
    o_sbk = o_sbq + SB_WIDTH
    o_sbv = o_sbk + SB_WIDTH
    o_kr = o_sbv + SB_WIDTH
    q_lat = proj[:, :o_kv]
    kv_lat = proj[:, o_kv:o_sbq]
    sbq = proj[:, o_sbq:o_sbk]
    sbk = proj[:, o_sbk:o_sbv]
    sbv = proj[:, o_sbv:o_kr]
    kr = proj[:, o_kr:o_kr + HEAD_PAD]

    cos_t = cos_ref[rs, :]
    sin_t = sin_ref[rs, :]

    sbq_ref[rs, :] = (sbq * SB_SCALE).astype(BF16)
    sbkb_ref[rs, :] = sbk.astype(BF16)
    sbvb_ref[rs, :] = sbv.astype(BF16)
    sbk_ref[rs, :] = sbk
    sbv_ref[rs, :] = sbv

    q = _dot(_rms(q_lat, gql_ref[...]).astype(BF16), wuq_ref[...])
    gq = gq_ref[...]
    for hd in range(N_HEADS):
        qh = q[:, hd * HEAD_PAD:(hd + 1) * HEAD_PAD]
        inv = lax.rsqrt(jnp.sum(qh * qh, axis=-1, keepdims=True) * (1.0 / QK) + EPS)
        qh = _rope128(qh * inv * gq, cos_t, sin_t)
        qhb = qh.astype(BF16)
        q_ref[rs, hd * HEAD_PAD:(hd + 1) * HEAD_PAD] = qhb
        if maybe_qabs_ref:
            wk_h = wuk_ref[:, hd * HEAD_PAD:(hd + 1) * HEAD_PAD]
            maybe_qabs_ref[0][rs, hd * KV_LORA:(hd + 1) * KV_LORA] = _dot_nt(qhb, wk_h).astype(BF16)

    ckv = _rms(kv_lat, gkv_ref[...])
    ckv_ref[rs, :] = ckv
    cb = ckv.astype(BF16)
    kn = _dot(cb, wuk_ref[...])
    v_ref[rs, :] = _dot(cb, wuv_ref[...]).astype(BF16)
    kpe = _rope128(kr * gkr_ref[...], cos_t, sin_t)
    kpe_ref[rs, :] = kpe
    lane = lax.broadcasted_iota(I32, kpe.shape, 1)
    kinv_all = jnp.zeros_like(kpe)
    for hd in range(N_HEADS):
        knh = kn[:, hd * HEAD_PAD:(hd + 1) * HEAD_PAD]
        kp = knh + kr
        inv = lax.rsqrt(jnp.sum(kp * kp, axis=-1, keepdims=True) * (1.0 / QK) + EPS)
        k_ref[rs, hd * HEAD_PAD:(hd + 1) * HEAD_PAD] = ((knh + kpe) * (inv * (MLA_SCALE * LOG2E))).astype(BF16)
        kinv_all = jnp.where(lane == hd, inv, kinv_all)
    kinv_ref[rs, :] = kinv_all


def _pre_project(x, shift, scale, cos_t, sin_t, wts, *, tm, per_token_mod, pos_tiles, with_qabs):
    t = x.shape[0]
    n_tiles = t // tm
    if per_token_mod:
        mod_spec = pl.BlockSpec((1, tm, D_MODEL), lambda i: (i, 0, 0))
    else:
        tiles_per_seq = n_tiles // shift.shape[0]
        mod_spec = pl.BlockSpec((1, 1, D_MODEL), lambda i: (i // tiles_per_seq, 0, 0))
    tab_spec = pl.BlockSpec((tm, HEAD_PAD), lambda i: (i % pos_tiles, 0))

    def full(a):
        return pl.BlockSpec(a.shape, lambda i: (0,) * a.ndim)

    def rows(width):
        return pl.BlockSpec((tm, width), lambda i: (i, 0))

    w_in_w = wts["w_in"].shape[1]
    hp = N_HEADS * HEAD_PAD
    out_shapes = [
        jax.ShapeDtypeStruct((t, hp), BF16), jax.ShapeDtypeStruct((t, hp), BF16),
        jax.ShapeDtypeStruct((t, hp), BF16),
        jax.ShapeDtypeStruct((t, SB_WIDTH), BF16), jax.ShapeDtypeStruct((t, SB_WIDTH), BF16),
        jax.ShapeDtypeStruct((t, SB_WIDTH), BF16),
        jax.ShapeDtypeStruct((t, KV_LORA), F32), jax.ShapeDtypeStruct((t, HEAD_PAD), F32),
        jax.ShapeDtypeStruct((t, HEAD_PAD), F32),
        jax.ShapeDtypeStruct((t, SB_WIDTH), F32), jax.ShapeDtypeStruct((t, SB_WIDTH), F32),
    ]
    out_specs = [rows(hp), rows(hp), rows(hp), rows(SB_WIDTH), rows(SB_WIDTH), rows(SB_WIDTH),
                 rows(KV_LORA), rows(HEAD_PAD), rows(HEAD_PAD), rows(SB_WIDTH), rows(SB_WIDTH)]
    if with_qabs:
        out_shapes.append(jax.ShapeDtypeStruct((t, N_HEADS * KV_LORA), BF16))
        out_specs.append(rows(N_HEADS * KV_LORA))
    del w_in_w
    return pl.pallas_call(
        _pre_kernel,
        grid=(n_tiles,),
        in_specs=[rows(D_MODEL), mod_spec, mod_spec, tab_spec, tab_spec,
                  full(wts["g_attn"]), full(wts["g_q_lat"]), full(wts["g_kv_lat"]), full(wts["gq"]),
                  full(wts["gkr"]), full(wts["w_in"]), full(wts["w_uq"]), full(wts["w_uk"]),
                  full(wts["w_uv"])],
        out_specs=out_specs,
        out_shape=out_shapes,
        compiler_params=_params(("arbitrary",)),
        name="pre_project",
    )(x, shift, scale, cos_t, sin_t, wts["g_attn"], wts["g_q_lat"], wts["g_kv_lat"], wts["gq"],
      wts["gkr"], wts["w_in"], wts["w_uq"], wts["w_uk"], wts["w_uv"])


def _mla_prompt_kernel(q_ref, k_ref, vt_ref, o_ref, *, blk, hg):
    qi = pl.program_id(2)
    qs = [q_ref[:, g * HEAD_PAD:(g + 1) * HEAD_PAD] for g in range(hg)]

    def step(j, carry, diagonal):
        start = pl.multiple_of(j * blk, blk)
        out = []
        for g in range(hg):
            m, l, acc = carry[g]
            k = k_ref[pl.ds(start, blk), g * HEAD_PAD:(g + 1) * HEAD_PAD]
            vt = vt_ref[j, g * HEAD_PAD:(g + 1) * HEAD_PAD, :]
            s = _dot_nt(k, qs[g])
            if diagonal:
                key = lax.broadcasted_iota(I32, s.shape, 0)
                qry = lax.broadcasted_iota(I32, s.shape, 1)
                s = jnp.where(key <= qry, s, -jnp.inf)
            m_new = jnp.maximum(m, jnp.max(s, axis=0, keepdims=True))
            alpha = jnp.exp2(m - m_new)
            p = jnp.exp2(s - m_new)
            l = alpha * l + jnp.sum(p, axis=0, keepdims=True)
            acc = alpha * acc + _dot(vt, p.astype(BF16))
            out.append((m_new, l, acc))
        return tuple(out)

    init = tuple((jnp.full((1, blk), -jnp.inf, F32), jnp.zeros((1, blk), F32),
                  jnp.zeros((HEAD_PAD, blk), F32)) for _ in range(hg))
    carry = lax.fori_loop(0, qi, lambda j, c: step(j, c, False), init)
    carry = step(qi, carry, True)
    for g in range(hg):
        _, l, acc = carry[g]
        o_ref[0, g * HEAD_PAD:(g + 1) * HEAD_PAD, :] = (acc / l).astype(BF16)


def _mla_prompt(q, k, v, batch, seq, blk, hg):
    nq = seq // blk
    width = hg * HEAD_PAD
    hp = N_HEADS * HEAD_PAD
    vt = v.reshape(batch * nq, blk, hp).transpose(0, 2, 1)
    ot = pl.pallas_call(
        functools.partial(_mla_prompt_kernel, blk=blk, hg=hg),
        grid=(batch, N_HEADS // hg, nq),
        in_specs=[pl.BlockSpec((blk, width), lambda b, h, i: (b * nq + i, h)),
                  pl.BlockSpec((seq, width), lambda b, h, i: (b, h)),
                  pl.BlockSpec((nq, width, blk), lambda b, h, i: (b, h, 0))],
        out_specs=pl.BlockSpec((1, width, blk), lambda b, h, i: (b * nq + i, h, 0)),
        out_shape=jax.ShapeDtypeStruct((batch * nq, hp, blk), BF16),
        compiler_params=_params(("arbitrary", "arbitrary", "arbitrary")),
        name="mla_prompt",
    )(q, k, vt)
    return ot.transpose(0, 2, 1).reshape(batch * seq, hp)


def _sb_prompt_kernel(q_ref, k_ref, v_ref, o_ref, *, blk):
    qi = pl.program_id(2)
    q2 = q_ref[...]
    lane = lax.broadcasted_iota(I32, q2.shape, 1)
    row = lax.broadcasted_iota(I32, (blk, blk), 0)
    col = lax.broadcasted_iota(I32, (blk, blk), 1)
    later_mat = (row > col).astype(BF16)
    zero = jnp.zeros_like(q2)
    qms = [jnp.where(lane < SB_DIM, q2, zero), jnp.where(lane >= SB_DIM, q2, zero)]

    def cond(c):
        return jnp.logical_and(c[0] >= 0, c[1] == 0)

    def body(c):
        j, _, tails, accs = c
        start = pl.multiple_of(j * blk, blk)
        k = k_ref[pl.ds(start, blk), :]
        v = v_ref[pl.ds(start, blk), :]
        mask = (col + j * blk) < (row + qi * blk)
        new_tails, new_accs = [], []
        for hh in range(2):
            z = _dot_nt(qms[hh], k)
            sp = _softplus(z)
            lf = jnp.where(mask, -sp, 0.0)
            hi, lo = _split_bf16(lf)
            later = _dot(hi, later_mat) + _dot(lo, later_mat)
            a = jnp.where(mask, jnp.exp((z - sp) + later + tails[hh]), 0.0)
            new_accs.append(accs[hh] + _dot(a.astype(BF16), v))
            new_tails.append(tails[hh] + jnp.sum(lf, axis=1, keepdims=True))
        most = jnp.max(jnp.maximum(new_tails[0], new_tails[1]))
        return j - 1, (most < SB_DONE).astype(I32), tuple(new_tails), tuple(new_accs)

    zt = jnp.zeros((blk, 1), F32)
    za = jnp.zeros((blk, 2 * SB_DIM), F32)
    accs = lax.while_loop(cond, body, (qi, jnp.int32(0), (zt, zt), (za, za)))[3]
    o_ref[...] = jnp.where(lane < SB_DIM, accs[0], accs[1]).astype(BF16)


def _sb_prompt(q, k, v, batch, seq, blk):
    nq = seq // blk
    pairs = SB_WIDTH // LANES
    return pl.pallas_call(
        functools.partial(_sb_prompt_kernel, blk=blk),
        grid=(batch, pairs, nq),
        in_specs=[pl.BlockSpec((blk, LANES), lambda b, h, i: (b * nq + i, h)),
                  pl.BlockSpec((seq, LANES), lambda b, h, i: (b, h)),
                  pl.BlockSpec((seq, LANES), lambda b, h, i: (b, h))],
        out_specs=pl.BlockSpec((blk, LANES), lambda b, h, i: (b * nq + i, h)),
        out_shape=jax.ShapeDtypeStruct(q.shape, BF16),
        compiler_params=_params(("arbitrary", "arbitrary", "arbitrary")),
        name="sb_prompt",
    )(q, k, v)


def _mla_decode_kernel(pt_ref, qabs_ref, qrope_ref, cnew_ref, penew_ref, invnew_ref,
                       ckv_hbm, kpe_hbm, kinv_hbm, o_ref, ckv_buf, kpe_buf, kinv_buf, sem,
                       *, layer, n_pg, n_chunks):
    b = pl.program_id(0)
    total = pl.num_programs(0) * n_chunks
    qa = qabs_ref[0]
    qr = qrope_ref[0]

    def copies(seq, chunk, slot):
        out = []
        for p in range(n_pg):
            page = pt_ref[seq, chunk * n_pg + p]
            out.append(pltpu.make_async_copy(ckv_hbm.at[layer, page], ckv_buf.at[slot, p], sem.at[slot, 0]))
            out.append(pltpu.make_async_copy(kpe_hbm.at[layer, page], kpe_buf.at[slot, p], sem.at[slot, 1]))
            out.append(pltpu.make_async_copy(kinv_hbm.at[layer, page], kinv_buf.at[slot, p], sem.at[slot, 2]))
        return out

    def start(g):
        @pl.when(g < total)
        def _():
            for c in copies(g // n_chunks, g % n_chunks, g % DECODE_SLOTS):
                c.start()

    def wait(slot):
        for c in copies(0, 0, slot):
            c.wait()

    @pl.when(b == 0)
    def _():
        for g in range(DECODE_DEPTH):
            start(jnp.int32(g))

    c_new = cnew_ref[0]
    s0 = (jnp.sum(qa.astype(F32)[:N_HEADS] * c_new, axis=1, keepdims=True)
          + jnp.sum(qr.astype(F32)[:N_HEADS] * penew_ref[0], axis=1, keepdims=True))
    init = (s0 * invnew_ref[0] * MLA_SCALE, jnp.ones((N_HEADS, 1), F32),
            jnp.broadcast_to(c_new, (N_HEADS, KV_LORA)))

    def chunk_step(slot, carry):
        m_old, l, acc = carry
        cb = ckv_buf[slot].reshape(n_pg * PAGE, KV_LORA).astype(BF16)
        kpe_t = jnp.concatenate([kpe_buf[slot, p] for p in range(n_pg)], axis=1)
        kinv_t = jnp.concatenate([kinv_buf[slot, p] for p in range(n_pg)], axis=1)
        s = (_dot_nt(qa, cb) + _dot(qr, kpe_t.astype(BF16)))[:N_HEADS]
        s = s * kinv_t * MLA_SCALE
        m_new = jnp.maximum(m_old, jnp.max(s, axis=1, keepdims=True))
        alpha = jnp.exp(m_old - m_new)
        p = jnp.exp(s - m_new)
        l = alpha * l + jnp.sum(p, axis=1, keepdims=True)
        p16 = jnp.concatenate([p, jnp.zeros_like(p)], axis=0).astype(BF16)
        acc = alpha * acc + _dot(p16, cb)[:N_HEADS]
        return m_new, l, acc

    def chunk(c, carry):
        g = b * n_chunks + c
        start(g + DECODE_DEPTH)
        slot = g % DECODE_SLOTS
        wait(slot)
        return chunk_step(slot, carry)

    _, l, acc = lax.fori_loop(0, n_chunks, chunk, init)
    o_ref[0] = acc / l


def _mla_decode(layer, page_table, qabs, qrope, c_new, pe_new, inv_new, cache_ckv, cache_kpe_t, cache_kinv_t,
                n_pg):
    nb, n_pages = page_table.shape
    assert n_pages % n_pg == 0
    hr = 2 * N_HEADS
    anyspec = pl.BlockSpec(memory_space=pl.ANY)
    return pl.pallas_call(
        functools.partial(_mla_decode_kernel, layer=layer, n_pg=n_pg, n_chunks=n_pages // n_pg),
        grid_spec=pltpu.PrefetchScalarGridSpec(
            num_scalar_prefetch=1,
            grid=(nb,),
            in_specs=[pl.BlockSpec((1, hr, KV_LORA), lambda b, pt: (b, 0, 0)),
                      pl.BlockSpec((1, hr, ROPE), lambda b, pt: (b, 0, 0)),
                      pl.BlockSpec((1, 1, KV_LORA), lambda b, pt: (b, 0, 0)),
                      pl.BlockSpec((1, 1, ROPE), lambda b, pt: (b, 0, 0)),
                      pl.BlockSpec((1, N_HEADS, 1), lambda b, pt: (b, 0, 0)),
                      anyspec, anyspec, anyspec],
            out_specs=pl.BlockSpec((1, N_HEADS, KV_LORA), lambda b, pt: (b, 0, 0)),
            scratch_shapes=[pltpu.VMEM((DECODE_SLOTS, n_pg, PAGE, KV_LORA), F32),
                            pltpu.VMEM((DECODE_SLOTS, n_pg, ROPE, PAGE), F32),
                            pltpu.VMEM((DECODE_SLOTS, n_pg, N_HEADS, PAGE), F32),
                            pltpu.SemaphoreType.DMA((DECODE_SLOTS, 3))]),
        out_shape=jax.ShapeDtypeStruct((nb, N_HEADS, KV_LORA), F32),
        compiler_params=_params(("arbitrary",)),
        name="mla_decode",
    )(page_table, qabs, qrope, c_new, pe_new, inv_new, cache_ckv, cache_kpe_t, cache_kinv_t)


def _sb_page(q_ref, kt_ref, vt_ref, tail, accs):
    head = lax.broadcasted_iota(I32, (N_HEADS, PAGE), 0)
    lane = lax.broadcasted_iota(I32, (N_HEADS, PAGE), 1)
    z = jnp.zeros((N_HEADS, PAGE), F32)
    for h in range(N_HEADS):
        zh = jnp.sum(kt_ref[h] * q_ref[h], axis=0, keepdims=True)
        z = jnp.where(head == h, zh, z)
    sp = _softplus(z)
    lf = -sp
    suf = lf
    shift = 1
    while shift < PAGE:
        moved = pltpu.roll(suf, PAGE - shift, 1)
        suf = suf + jnp.where(lane < PAGE - shift, moved, 0.0)
        shift *= 2
    a = jnp.exp((z - sp) + (suf - lf) + tail)
    new_accs = []
    for h in range(N_HEADS):
        new_accs.append(accs[h] + jnp.sum(vt_ref[h] * a[h:h + 1, :], axis=1, keepdims=True))
    return tail + suf[:, 0:1], new_accs


def _sb_decode_kernel(pt_ref, q_ref, k0_ref, v0_ref, k1_ref, v1_ref, kc_ref, vc_ref, o_ref,
                      kbuf, vbuf, sem, *, layer, n_pages, n_pre):
    b = pl.program_id(0)
    q = q_ref.at[0]
    tail = jnp.zeros((N_HEADS, 1), F32)
    accs = [jnp.zeros((SB_DIM, 1), F32) for _ in range(N_HEADS)]
    tail, accs = _sb_page(q, k0_ref.at[0, 0], v0_ref.at[0, 0], tail, accs)
    if n_pre > 1:
        tail, accs = _sb_page(q, k1_ref.at[0, 0], v1_ref.at[0, 0], tail, accs)

    def cond(c):
        return jnp.logical_and(c[0] >= 0, c[1] == 0)

    def body(c):
        j, _, tail, accs = c
        page = pt_ref[b, j]
        ck = pltpu.make_async_copy(kc_ref.at[layer, page], kbuf, sem.at[0])
        cv = pltpu.make_async_copy(vc_ref.at[layer, page], vbuf, sem.at[1])
        ck.start()
        cv.start()
        ck.wait()
        cv.wait()
        tail, accs = _sb_page(q, kbuf, vbuf, tail, list(accs))
        return j - 1, (jnp.max(tail) < SB_DONE).astype(I32), tail, tuple(accs)

    done0 = (jnp.max(tail) < SB_DONE).astype(I32)
    accs = lax.while_loop(cond, body, (jnp.int32(n_pages - 1 - n_pre), done0, tail, tuple(accs)))[3]
    for h in range(N_HEADS):
        o_ref[0, h] = accs[h]


def _sb_decode(layer, page_table, q, cache_kt, cache_vt):
    nb, n_pages = page_table.shape
    n_pre = 2 if n_pages >= 2 else 1
    page_shape = (N_HEADS, SB_DIM, PAGE)

    def pg(back):
        return pl.BlockSpec((1, 1) + page_shape, lambda b, pt: (layer, pt[b, n_pages - 1 - back], 0, 0, 0))

    qspec = pl.BlockSpec((1, N_HEADS, SB_DIM, 1), lambda b, pt: (b, 0, 0, 0))
    return pl.pallas_call(
        functools.partial(_sb_decode_kernel, layer=layer, n_pages=n_pages, n_pre=n_pre),
        grid_spec=pltpu.PrefetchScalarGridSpec(
            num_scalar_prefetch=1,
            grid=(nb,),
            in_specs=[qspec, pg(0), pg(0), pg(n_pre - 1), pg(n_pre - 1),
                      pl.BlockSpec(memory_space=pl.ANY), pl.BlockSpec(memory_space=pl.ANY)],
            out_specs=qspec,
            scratch_shapes=[pltpu.VMEM(page_shape, F32), pltpu.VMEM(page_shape, F32),
                            pltpu.SemaphoreType.DMA((2,))]),
        out_shape=jax.ShapeDtypeStruct((nb, N_HEADS, SB_DIM, 1), F32),
        compiler_params=_params(("arbitrary",)),
        name="sb_decode",
    )(page_table, q, cache_kt, cache_vt, cache_kt, cache_vt, cache_kt, cache_vt)


def _uv_kernel(o_ref, w_ref, out_ref):
    for hd in range(N_HEADS):
        out_ref[:, hd * HEAD_PAD:(hd + 1) * HEAD_PAD] = _dot(
            o_ref[:, hd * KV_LORA:(hd + 1) * KV_LORA].astype(BF16), w_ref[:, hd * HEAD_PAD:(hd + 1) * HEAD_PAD]
        ).astype(BF16)


def _value_up(o_lat, w_uv_p):
    t = o_lat.shape[0]
    return pl.pallas_call(
        _uv_kernel,
        out_shape=jax.ShapeDtypeStruct((t, N_HEADS * HEAD_PAD), BF16),
        compiler_params=pltpu.CompilerParams(vmem_limit_bytes=VMEM_LIMIT),
        name="value_up",
    )(o_lat, w_uv_p)


def _post_kernel(x_ref, om_ref, os_ref, ga_ref, shm_ref, scm_ref, gm_ref, gmoe_ref, wom_ref, wos_ref,
                 wgu_ref, wd_ref, base_ref, h2_ref):
    attn = _dot(om_ref[...], wom_ref[...]) + _dot(os_ref[...], wos_ref[...])
    xm = x_ref[...] + ga_ref[0] * attn
    h2 = _rms(xm, gmoe_ref[...]) * (1.0 + scm_ref[0]) + shm_ref[0]
    h2_ref[...] = h2
    gu = _dot(h2.astype(BF16), wgu_ref[...])
    g = gu[:, :D_SHARED]
    act = (g * _sigmoid(g)) * gu[:, D_SHARED:]
    ysh = _dot(act.astype(BF16), wd_ref[...])
    base_ref[...] = xm + gm_ref[0] * ysh


def _post_attention(x, o_mla, o_sb, mods, wts, *, tm, per_token_mod):
    t = x.shape[0]
    n_tiles = t // tm
    g_a, sh_m, sc_m, g_m = mods
    if per_token_mod:
        mod_spec = pl.BlockSpec((1, tm, D_MODEL), lambda i: (i, 0, 0))
    else:
        tiles_per_seq = n_tiles // g_a.shape[0]
        mod_spec = pl.BlockSpec((1, 1, D_MODEL), lambda i: (i // tiles_per_seq, 0, 0))

    def full(a):
        return pl.BlockSpec(a.shape, lambda i: (0,) * a.ndim)

    def rows(width):
        return pl.BlockSpec((tm, width), lambda i: (i, 0))

    buf = jax.ShapeDtypeStruct((t, D_MODEL), F32)
    return pl.pallas_call(
        _post_kernel,
        grid=(n_tiles,),
        in_specs=[rows(D_MODEL), rows(N_HEADS * HEAD_PAD), rows(SB_WIDTH), mod_spec, mod_spec, mod_spec,
                  mod_spec, full(wts["g_moe"]), full(wts["w_o_mla"]), full(wts["w_o_sb"]),
                  full(wts["w_sh_gu"]), full(wts["w_sh_down"])],
        out_specs=[rows(D_MODEL), rows(D_MODEL)],
        out_shape=[buf, buf],
        compiler_params=_params(("arbitrary",)),
        name="post_attention",
    )(x, o_mla, o_sb, g_a, sh_m, sc_m, g_m, wts["g_moe"], wts["w_o_mla"], wts["w_o_sb"],
      wts["w_sh_gu"], wts["w_sh_down"])


def _router_kernel(hp_ref, hs_ref, whi_ref, wlo_ref, b_ref, idx_ref, wgt_ref, rank_ref, cnt_ref, *,
                   n_prompt_tiles):
    i = pl.program_id(0)

    @pl.when(i == 0)
    def _():
        cnt_ref[...] = jnp.zeros_like(cnt_ref)

    h = jnp.where(i < n_prompt_tiles, hp_ref[...], hs_ref[...])
    tm = h.shape[0]
    hhi, hlo = _split_bf16(h)
    whi = whi_ref[...]
    logits = _dot_nt(whi, hhi) + _dot_nt(whi, hlo) + _dot_nt(wlo_ref[...], hhi)
    scores = _sigmoid(logits)
    sel = scores + b_ref[...]
    neg = -jnp.inf

    blocks = [sel[g * GROUP_SIZE:(g + 1) * GROUP_SIZE, :] for g in range(N_GROUPS)]
    gs = []
    for blk in blocks:
        m1 = jnp.max(blk, axis=0, keepdims=True)
        is_max = blk == m1
        n_max = jnp.sum(is_max.astype(F32), axis=0, keepdims=True)
        m2 = jnp.max(jnp.where(is_max, neg, blk), axis=0, keepdims=True)
        gs.append(m1 + jnp.where(n_max >= 2.0, m1, m2))
    kept = []
    for g in range(N_GROUPS):
        ahead = jnp.zeros_like(gs[g])
        for g2 in range(N_GROUPS):
            if g2 == g:
                continue
            better = gs[g2] > gs[g]
            if g2 < g:
                better = jnp.logical_or(better, gs[g2] == gs[g])
            ahead = ahead + better.astype(F32)
        kept.append(jnp.where(ahead < float(TOPK_GROUPS), blocks[g], neg))
    cur = jnp.concatenate(kept, axis=0)

    rowf = lax.broadcasted_iota(I32, cur.shape, 0).astype(F32)
    chosen = jnp.zeros_like(cur)
    idxs, ws = [], []
    for _ in range(TOP_K):
        m = jnp.max(cur, axis=0, keepdims=True)
        ik = jnp.min(jnp.where(cur == m, rowf, float(N_EXPERTS)), axis=0, keepdims=True)
        hit = rowf == ik
        ws.append(jnp.sum(jnp.where(hit, scores, 0.0), axis=0, keepdims=True))
        cur = jnp.where(hit, neg, cur)
        chosen = chosen + hit.astype(F32)
        idxs.append(ik)
    wsum = ws[0]
    for w in ws[1:]:
        wsum = wsum + w
    ws = [w / wsum * ROUTE_SCALE for w in ws]

    r2 = lax.broadcasted_iota(I32, (tm, tm), 0)
    c2 = lax.broadcasted_iota(I32, (tm, tm), 1)
    left = (r2 < c2).astype(BF16)
    pos = _dot(chosen.astype(BF16), left) + cnt_ref[...]
    ranks = [jnp.sum(jnp.where(rowf == ik, pos, 0.0), axis=0, keepdims=True) for ik in idxs]
    cnt_ref[...] = cnt_ref[...] + jnp.sum(chosen, axis=1, keepdims=True)

    idx_ref[...] = jnp.concatenate(idxs, axis=0).astype(I32)
    wgt_ref[...] = jnp.concatenate(ws, axis=0)
    rank_ref[...] = jnp.concatenate(ranks, axis=0).astype(I32)


def _route(h2_p, h2_s, whi, wlo, b_col, tm):
    n_p = h2_p.shape[0] // tm
    t = h2_p.shape[0] + h2_s.shape[0]
    sm = pl.BlockSpec((TOP_K, tm), lambda i: (0, i))
    return pl.pallas_call(
        functools.partial(_router_kernel, n_prompt_tiles=n_p),
        grid=(t // tm,),
        in_specs=[pl.BlockSpec((tm, D_MODEL), lambda i: (jnp.minimum(i, n_p - 1), 0)),
                  pl.BlockSpec((tm, D_MODEL), lambda i: (jnp.maximum(i - n_p, 0), 0)),
                  pl.BlockSpec(whi.shape, lambda i: (0, 0)),
                  pl.BlockSpec(wlo.shape, lambda i: (0, 0)),
                  pl.BlockSpec(b_col.shape, lambda i: (0, 0))],
        out_specs=[sm, sm, sm, pl.BlockSpec((N_EXPERTS, 1), lambda i: (0, 0))],
        out_shape=[jax.ShapeDtypeStruct((TOP_K, t), I32), jax.ShapeDtypeStruct((TOP_K, t), F32),
                   jax.ShapeDtypeStruct((TOP_K, t), I32), jax.ShapeDtypeStruct((N_EXPERTS, 1), F32)],
        compiler_params=_params(("arbitrary",)),
        name="router",
    )(h2_p, h2_s, whi, wlo, b_col)


def _row(ref, r):
    return ref.at[pl.ds(r, 1), :]


def _dispatch_kernel(dest_ref, hp_ref, hs_ref, xs_ref, sem, *, tm, n_prompt_tiles):
    i = pl.program_id(0)

    def scatter_rows(src_ref):
        def issue(t, carry):
            for k in range(TOP_K):
                pltpu.make_async_copy(_row(src_ref, t), _row(xs_ref, dest_ref[k, t]), sem).start(priority=k % 2)
            return carry

        lax.fori_loop(0, tm, issue, 0)

        def drain(t, carry):
            for _ in range(TOP_K):
                pltpu.make_async_copy(_row(src_ref, 0), _row(xs_ref, 0), sem).wait()
            return carry

        lax.fori_loop(0, tm, drain, 0)

    @pl.when(i < n_prompt_tiles)
    def _():
        scatter_rows(hp_ref)

    @pl.when(i >= n_prompt_tiles)
    def _():
        scatter_rows(hs_ref)


def _dispatch(dest, h2_p, h2_s, n_rows, tm):
    n_p = h2_p.shape[0] // tm
    t = h2_p.shape[0] + h2_s.shape[0]
    return pl.pallas_call(
        functools.partial(_dispatch_kernel, tm=tm, n_prompt_tiles=n_p),
        grid=(t // tm,),
        in_specs=[pl.BlockSpec((TOP_K, tm), lambda i: (0, i), memory_space=pltpu.SMEM),
                  pl.BlockSpec((tm, D_MODEL), lambda i: (jnp.minimum(i, n_p - 1), 0)),
                  pl.BlockSpec((tm, D_MODEL), lambda i: (jnp.maximum(i - n_p, 0), 0))],
        out_specs=pl.BlockSpec(memory_space=pl.ANY),
        scratch_shapes=[pltpu.SemaphoreType.DMA(())],
        out_shape=jax.ShapeDtypeStruct((n_rows, D_MODEL), F32),
        compiler_params=_params(("arbitrary",)),
        name="dispatch",
    )(dest, h2_p, h2_s)


def _expert_kernel(be_ref, nb_ref, xs_ref, wg_ref, wu_ref, wd_ref, ys_ref, wgu_s, wd_s):
    i = pl.program_id(0)

    @pl.when(i < nb_ref[0])
    def _():
        prev = be_ref[jnp.maximum(i - 1, 0)]

        @pl.when(jnp.logical_or(i == 0, be_ref[i] != prev))
        def _():
            wgu_s[:, :D_EXPERT] = wg_ref[0].astype(BF16)
            wgu_s[:, D_EXPERT:] = wu_ref[0].astype(BF16)
            wd_s[...] = wd_ref[0].astype(BF16)

        gu = _dot(xs_ref[...].astype(BF16), wgu_s[...])
        g = gu[:, :D_EXPERT]
        act = (g * _sigmoid(g)) * gu[:, D_EXPERT:]
        ys_ref[...] = _dot(act.astype(BF16), wd_s[...])


def _experts(layer, block_e, n_used, xs, w_gate, w_up, w_down):
    n_rows = xs.shape[0]
    n_blocks = n_rows // EXPERT_BLOCK
    w_gate, w_up, w_down = (w.reshape((-1,) + w.shape[2:]) for w in (w_gate, w_up, w_down))

    def row_map(i, be, nb):
        return (jnp.minimum(i, nb[0] - 1), 0)

    def w_map(i, be, nb):
        return (layer * N_EXPERTS + be[jnp.minimum(i, nb[0] - 1)], 0, 0)

    return pl.pallas_call(
        _expert_kernel,
        grid_spec=pltpu.PrefetchScalarGridSpec(
            num_scalar_prefetch=2,
            grid=(n_blocks,),
            in_specs=[pl.BlockSpec((EXPERT_BLOCK, D_MODEL), row_map),
                      pl.BlockSpec((1, D_MODEL, D_EXPERT), w_map),
                      pl.BlockSpec((1, D_MODEL, D_EXPERT), w_map),
                      pl.BlockSpec((1, D_EXPERT, D_MODEL), w_map)],
            out_specs=pl.BlockSpec((EXPERT_BLOCK, D_MODEL), row_map),
            scratch_shapes=[pltpu.VMEM((D_MODEL, 2 * D_EXPERT), BF16), pltpu.VMEM((D_EXPERT, D_MODEL), BF16)]),
        out_shape=jax.ShapeDtypeStruct(xs.shape, F32),
        compiler_params=_params(("arbitrary",)),
        name="experts",
    )(block_e, n_used, xs, w_gate, w_up, w_down)


def _combine_kernel(dest_ref, dest_next_ref, wgt_ref, ys_ref, basep_ref, bases_ref, gmp_ref, gms_ref,
                    outp_ref, outs_ref, buf, sem, *, tm, n_prompt_tiles):
    i = pl.program_id(0)
    slot = i % 2

    def gather_rows(rows_ref, into):
        def issue(t, carry):
            for k in range(TOP_K):
                pltpu.make_async_copy(_row(ys_ref, rows_ref[k, t]), _row(buf.at[into, k], t),
                                      sem.at[into]).start(priority=k % 2)
            return carry

        lax.fori_loop(0, tm, issue, 0)

    @pl.when(i == 0)
    def _():
        gather_rows(dest_ref, 0)

    @pl.when(i + 1 < pl.num_programs(0))
    def _():
        gather_rows(dest_next_ref, 1 - slot)

    def drain(t, carry):
        for k in range(TOP_K):
            pltpu.make_async_copy(_row(ys_ref, 0), _row(buf.at[slot, k], 0), sem.at[slot]).wait()
        return carry

    lax.fori_loop(0, tm, drain, 0)

    w = wgt_ref[...]
    mix = w[:, 0:1] * buf[slot, 0]
    for k in range(1, TOP_K):
        mix = mix + w[:, k:k + 1] * buf[slot, k]

    @pl.when(i < n_prompt_tiles)
    def _():
        outp_ref[...] = basep_ref[...] + gmp_ref[0] * mix

    @pl.when(i >= n_prompt_tiles)
    def _():
        outs_ref[...] = bases_ref[...] + gms_ref[...] * mix


def _combine(dest, wgt_t, ys, base_p, base_s, gm_prompt, gm_sample, tm, seq):
    t_all = base_p.shape[0] + base_s.shape[0]
    n_batch = gm_prompt.shape[0]
    n_prompt_tiles = n_batch * seq // tm
    tiles_per_seq = seq // tm
    tile = (tm, D_MODEL)
    prompt_tile = pl.BlockSpec(tile, lambda i: (jnp.minimum(i, n_prompt_tiles - 1), 0))
    sample_tile = pl.BlockSpec(tile, lambda i: (jnp.maximum(i - n_prompt_tiles, 0), 0))
    n_tiles = t_all // tm
    return pl.pallas_call(
        functools.partial(_combine_kernel, tm=tm, n_prompt_tiles=n_prompt_tiles),
        grid=(n_tiles,),
        in_specs=[pl.BlockSpec((TOP_K, tm), lambda i: (0, i), memory_space=pltpu.SMEM),
                  pl.BlockSpec((TOP_K, tm), lambda i: (0, jnp.minimum(i + 1, n_tiles - 1)), memory_space=pltpu.SMEM),
                  pl.BlockSpec((tm, TOP_K), lambda i: (i, 0)),
                  pl.BlockSpec(memory_space=pl.ANY), prompt_tile, sample_tile,
                  pl.BlockSpec((1, 1, D_MODEL), lambda i: (jnp.minimum(i // tiles_per_seq, n_batch - 1), 0, 0)),
                  sample_tile],
        out_specs=[prompt_tile, sample_tile],
        scratch_shapes=[pltpu.VMEM((2, TOP_K, tm, D_MODEL), F32), pltpu.SemaphoreType.DMA((2,))],
        out_shape=[jax.ShapeDtypeStruct(base_p.shape, F32), jax.ShapeDtypeStruct(base_s.shape, F32)],
        compiler_params=_params(("arbitrary",)),
        name="combine",
    )(dest, dest, wgt_t, ys, base_p, base_s, gm_prompt, gm_sample)


def _pad_heads(w, width):
    k = w.shape[0]
    w = w.reshape(k, N_HEADS, width)
    return jnp.pad(w, ((0, 0), (0, 0), (0, HEAD_PAD - width))).reshape(k, N_HEADS * HEAD_PAD)


def _rope_tables(pos):
    half = ROPE // 2
    inv_freq = ROPE_THETA ** (-jnp.arange(half, dtype=F32) * (2.0 / ROPE))
    ang = pos.astype(F32)[:, None] * inv_freq[None, :]
    cos, sin = jnp.cos(ang), jnp.sin(ang)
    n = pos.shape[0]
    ones = jnp.ones((n, NOPE), F32)
    zeros = jnp.zeros((n, HEAD_PAD - QK), F32)
    cos_t = jnp.concatenate([ones, cos, cos, zeros], axis=1)
    sin_t = jnp.concatenate([jnp.zeros((n, NOPE), F32), -sin, sin, zeros], axis=1)
    return cos_t, sin_t


def _layer_weights(l, w_ada, b_ada, g_attn, g_moe, w_in, g_q_lat, w_uq, g_kv_lat, w_uk, w_uv, g_qk_q, g_qk_k,
                   w_o, w_router, b_router, w_sh_gate, w_sh_up, w_sh_down):
    wi = w_in[l]
    o1 = Q_LORA + KV_LORA
    o2 = o1 + ROPE
    kr_block = jnp.zeros((D_MODEL, HEAD_PAD), F32).at[:, NOPE:QK].set(wi[:, o1:o2])
    w_in_p = jnp.concatenate([wi[:, :o1], wi[:, o2:], kr_block], axis=1).astype(BF16)
    gq = jnp.concatenate([g_qk_q[l][:NOPE] * g_qk_k[l][:NOPE], g_qk_q[l][NOPE:],
                          jnp.zeros((HEAD_PAD - QK,), F32)]).reshape(1, HEAD_PAD)
    gkr = jnp.concatenate([jnp.zeros((NOPE,), F32), g_qk_k[l][NOPE:],
                           jnp.zeros((HEAD_PAD - QK,), F32)]).reshape(1, HEAD_PAD)
    wo = w_o[l]
    n_mla = N_HEADS * NOPE
    w_o_mla = jnp.pad(wo[:n_mla].reshape(N_HEADS, NOPE, D_MODEL),
                      ((0, 0), (0, HEAD_PAD - NOPE), (0, 0))).reshape(N_HEADS * HEAD_PAD, D_MODEL)
    wr_t = w_router[l].T
    whi = wr_t.astype(BF16)
    wlo = (wr_t - whi.astype(F32)).astype(BF16)
    return dict(
        w_ada=w_ada[l], b_ada=b_ada[l],
        g_attn=g_attn[l].reshape(1, -1), g_moe=g_moe[l].reshape(1, -1),
        g_q_lat=g_q_lat[l].reshape(1, -1), g_kv_lat=g_kv_lat[l].reshape(1, -1), gq=gq, gkr=gkr,
        w_in=w_in_p, w_uq=_pad_heads(w_uq[l], QK).astype(BF16),
        w_uk=_pad_heads(w_uk[l], NOPE).astype(BF16), w_uv=_pad_heads(w_uv[l], NOPE).astype(BF16),
        w_o_mla=w_o_mla.astype(BF16), w_o_sb=wo[n_mla:].astype(BF16),
        w_r_hi=whi, w_r_lo=wlo, b_router=b_router[l].reshape(-1, 1),
        w_sh_gu=jnp.concatenate([w_sh_gate[l], w_sh_up[l]], axis=1).astype(BF16),
        w_sh_down=w_sh_down[l].astype(BF16),
    )


def _tile(n, pref):
    t = min(n, pref)
    while n % t:
        t //= 2
    return t


def _layer(l, xp, xs_, cache_mla_ckv, cache_mla_kpe, cache_mla_kinv, cache_sb_k, cache_sb_v, page_table,
           c_prompt, c_sample, wts, w_exp_gate, w_exp_up, w_exp_down):
    batch, seq, _ = xp.shape
    nb = xs_.shape[0]
    n_pages = page_table.shape[1]
    past = n_pages * PAGE
    t_p = batch * seq
    tok_tile = LANES
    assert xs_.shape[1] == 1 and nb % tok_tile == 0 and seq % tok_tile == 0
    t_s = nb
    tm_s = tok_tile
    t_all = t_p + t_s

    mods = _adaln(jnp.concatenate([c_prompt, c_sample], axis=0), wts["w_ada"], wts["b_ada"])
    mods_p = [m.reshape(batch, 1, D_MODEL) for m in jnp.split(mods[:batch], 6, axis=-1)]
    mods_s = [m.reshape(t_s // tm_s, tm_s, D_MODEL) for m in jnp.split(mods[batch:], 6, axis=-1)]

    tm_p = _tile(seq, 512)
    cos_p, sin_p = _rope_tables(jnp.arange(seq))
    (q, k, v, sbq, sbkb, sbvb, ckv_p, kpe_p, kinv_p, sbk_p, sbv_p) = _pre_project(
        xp.reshape(t_p, D_MODEL), mods_p[0], mods_p[1], cos_p, sin_p, wts,
        tm=tm_p, per_token_mod=False, pos_tiles=seq // tm_p, with_qabs=False)
    o_mla_p = _mla_prompt(q, k, v, batch, seq, _tile(seq, 512), 2)
    o_sb_p = _sb_prompt(sbq, sbkb, sbvb, batch, seq, _tile(seq, 256))

    cos_s, sin_s = _rope_tables(jnp.full((tm_s,), past, I32))
    (q_s, _, _, sbq_s, _, _, ckv_s, kpe_s, kinv_s, sbk_s, sbv_s, qabs_s) = _pre_project(
        xs_.reshape(t_s, D_MODEL), mods_s[0], mods_s[1], cos_s, sin_s, wts,
        tm=tm_s, per_token_mod=True, pos_tiles=1, with_qabs=True)
    head_rows = N_HEADS
    qabs3 = jnp.pad(qabs_s.reshape(t_s, N_HEADS, KV_LORA), ((0, 0), (0, head_rows), (0, 0)))
    qrope3 = jnp.pad(q_s.reshape(t_s, N_HEADS, HEAD_PAD)[:, :, NOPE:QK], ((0, 0), (0, head_rows), (0, 0)))
    n_pg = _tile(n_pages, 16)
    o_lat = _mla_decode(l, page_table, qabs3, qrope3, ckv_s.reshape(t_s, 1, KV_LORA),
                        kpe_s[:, NOPE:QK].reshape(t_s, 1, ROPE), kinv_s[:, :N_HEADS].reshape(t_s, N_HEADS, 1),
                        cache_mla_ckv, jnp.swapaxes(cache_mla_kpe, 2, 3), jnp.swapaxes(cache_mla_kinv, 2, 3),
                        n_pg)
    o_mla_s = _value_up(o_lat.reshape(t_s, N_HEADS * KV_LORA), wts["w_uv"])
    o_sb_s = _sb_decode(l, page_table, sbq_s.astype(F32).reshape(t_s, N_HEADS, SB_DIM, 1),
                        cache_sb_k.transpose(0, 1, 3, 4, 2), cache_sb_v.transpose(0, 1, 3, 4, 2))
    o_sb_s = o_sb_s.reshape(t_s, SB_WIDTH).astype(BF16)

    base_p, h2_p = _post_attention(xp.reshape(t_p, D_MODEL), o_mla_p, o_sb_p, mods_p[2:], wts,
                                   tm=tm_p, per_token_mod=False)
    base_s, h2_s = _post_attention(xs_.reshape(t_s, D_MODEL), o_mla_s, o_sb_s, mods_s[2:], wts,
                                   tm=tm_s, per_token_mod=True)

    tm_r = tok_tile
    idx, wgt, rank, counts = _route(h2_p, h2_s, wts["w_r_hi"], wts["w_r_lo"], wts["b_router"], tm_r)
    counts = counts.reshape(N_EXPERTS).astype(I32)
    padded = (counts + EXPERT_BLOCK - 1) // EXPERT_BLOCK * EXPERT_BLOCK
    pend = jnp.cumsum(padded)
    pstart = (pend - padded).astype(I32)
    n_assign = t_all * TOP_K
    n_rows = -(-n_assign // EXPERT_BLOCK) * EXPERT_BLOCK + N_EXPERTS * EXPERT_BLOCK
    n_blocks = n_rows // EXPERT_BLOCK
    block_start = jnp.arange(n_blocks, dtype=I32) * EXPERT_BLOCK
    block_e = jnp.minimum(jnp.sum((pend[None, :] <= block_start[:, None]).astype(I32), axis=1), N_EXPERTS - 1)
    n_used = (pend[-1:] // EXPERT_BLOCK).astype(I32)
    dest = jnp.sum(jnp.where(idx[:, :, None] == jnp.arange(N_EXPERTS, dtype=I32), pstart, 0), axis=-1) + rank
    xs_rows = _dispatch(dest, h2_p, h2_s, n_rows, tm_r)
    ys_rows = _experts(l, block_e, n_used, xs_rows, w_exp_gate, w_exp_up, w_exp_down)
    y_p, y_s = _combine(dest, wgt.T, ys_rows, base_p, base_s, mods_p[5],
                        mods_s[5].reshape(t_s, D_MODEL), tm_r, seq)
    y_p = y_p.reshape(batch, seq, D_MODEL)
    y_s = y_s.reshape(nb, 1, D_MODEL)

    new_p = (ckv_p.reshape(batch, seq, KV_LORA), kpe_p[:, NOPE:QK].reshape(batch, seq, ROPE),
             kinv_p[:, :N_HEADS].reshape(batch, seq, N_HEADS),
             sbk_p.reshape(batch, seq, N_HEADS, SB_DIM), sbv_p.reshape(batch, seq, N_HEADS, SB_DIM))
    new_s = (ckv_s.reshape(nb, 1, KV_LORA), kpe_s[:, NOPE:QK].reshape(nb, 1, ROPE),
             kinv_s[:, :N_HEADS].reshape(nb, 1, N_HEADS),
             sbk_s.reshape(nb, 1, N_HEADS, SB_DIM), sbv_s.reshape(nb, 1, N_HEADS, SB_DIM))
    return y_p, y_s, new_p, new_s


def kernel(x_prompt, x_sample, cache_mla_ckv, cache_mla_kpe, cache_mla_kinv, cache_sb_k, cache_sb_v, page_table, c_prompt, c_sample, w_ada, b_ada, g_attn, g_moe, w_in, g_q_lat, w_uq, g_kv_lat, w_uk, w_uv, g_qk_q, g_qk_k, w_o, w_router, b_router, w_exp_gate, w_exp_up, w_exp_down, w_sh_gate, w_sh_up, w_sh_down):
    depth = w_in.shape[0]
    y_p, y_s = x_prompt, x_sample
    p_rows = [[] for _ in range(5)]
    s_rows = [[] for _ in range(5)]
    for l in range(depth):
        wts = _layer_weights(l, w_ada, b_ada, g_attn, g_moe, w_in, g_q_lat, w_uq, g_kv_lat, w_uk, w_uv,
                             g_qk_q, g_qk_k, w_o, w_router, b_router, w_sh_gate, w_sh_up, w_sh_down)
        y_p, y_s, new_p, new_s = _layer(l, y_p, y_s, cache_mla_ckv, cache_mla_kpe, cache_mla_kinv,
                                        cache_sb_k, cache_sb_v, page_table, c_prompt, c_sample, wts,
                                        w_exp_gate, w_exp_up, w_exp_down)
        for lst, a in zip(p_rows, new_p):
            lst.append(a)
        for lst, a in zip(s_rows, new_s):
            lst.append(a)
    outs_p = [jnp.stack(r) for r in p_rows]
    outs_s = [jnp.stack(r) for r in s_rows]
    return (y_p, y_s, *outs_p, *outs_s)
```

```python
import functools

import jax
import jax.numpy as jnp
from jax import lax
from jax.experimental import pallas as pl
from jax.experimental.pallas import tpu as pltpu

F32 = jnp.float32
BF16 = jnp.bfloat16
I32 = jnp.int32

D_MODEL = 1024
PAGE = 128
N_HEADS = 8
NOPE = 64
ROPE = 32
QK = NOPE + ROPE
HEAD_PAD = 128
Q_LORA = 768
KV_LORA = 256
SB_DIM = 64
SB_WIDTH = N_HEADS * SB_DIM
ROPE_THETA = 10000.0
MLA_SCALE = QK ** -0.5
SB_SCALE = SB_DIM ** -0.5
LOG2E = 1.4426950408889634
N_EXPERTS = 256
TOP_K = 8
N_GROUPS = 8
GROUP_SIZE = N_EXPERTS // N_GROUPS
TOPK_GROUPS = 4
D_EXPERT = 256
D_SHARED = 256
ROUTE_SCALE = 2.5
EPS = 1e-6
LANES = 128
SB_DONE = -110.0
EXPERT_BLOCK = 256
DECODE_DEPTH = 2
DECODE_SLOTS = DECODE_DEPTH + 1
VMEM_LIMIT = 56 * 1024 * 1024

NT = (((1,), (1,)), ((), ()))


def _params(sem):
    return pltpu.CompilerParams(dimension_semantics=sem, vmem_limit_bytes=VMEM_LIMIT)


def _dot(a, b):
    return jnp.dot(a, b, preferred_element_type=F32)


def _dot_nt(a, b):
    return lax.dot_general(a, b, NT, preferred_element_type=F32)


def _split_bf16(x):
    hi = x.astype(BF16)
    lo = (x - hi.astype(F32)).astype(BF16)
    return hi, lo


def _sigmoid(x):
    return 1.0 / (1.0 + jnp.exp(-x))


def _softplus(x):
    return jnp.maximum(x, 0.0) + jnp.log(1.0 + jnp.exp(-jnp.abs(x)))


def _rms(x, g):
    return x * lax.rsqrt(jnp.mean(x * x, axis=-1, keepdims=True) + EPS) * g


def _rope128(x, cos_t, sin_t):
    lane = lax.broadcasted_iota(I32, x.shape, 1)
    up = pltpu.roll(x, LANES - ROPE // 2, 1)
    dn = pltpu.roll(x, ROPE // 2, 1)
    rot = jnp.where(lane < NOPE + ROPE // 2, up, dn)
    return x * cos_t + rot * sin_t


def _ada_kernel(c_ref, w_ref, b_ref, o_ref):
    c = c_ref[...]
    s = c * _sigmoid(c)
    o_ref[...] = _dot(s.astype(BF16), w_ref[...].astype(BF16)) + b_ref[...]


def _adaln(c, w_ada, b_ada):
    rows = c.shape[0]
    n = w_ada.shape[1]
    return pl.pallas_call(
        _ada_kernel,
        grid=(n // D_MODEL,),
        in_specs=[pl.BlockSpec((rows, D_MODEL), lambda j: (0, 0)),
                  pl.BlockSpec((D_MODEL, D_MODEL), lambda j: (0, j)),
                  pl.BlockSpec((1, D_MODEL), lambda j: (0, j))],
        out_specs=pl.BlockSpec((rows, D_MODEL), lambda j: (0, j)),
        out_shape=jax.ShapeDtypeStruct((rows, n), F32),
        compiler_params=_params(("arbitrary",)),
        name="adaln",
    )(c, w_ada, b_ada.reshape(1, n))


def _pre_kernel(x_ref, sh_ref, sc_ref, cos_ref, sin_ref, gattn_ref, gql_ref, gkv_ref, gq_ref, gkr_ref,
                win_ref, wuq_ref, wuk_ref, wuv_ref,
                q_ref, k_ref, v_ref, sbq_ref, sbkb_ref, sbvb_ref, ckv_ref, kpe_ref, kinv_ref,
                sbk_ref, sbv_ref, *maybe_qabs_ref):
    tm = x_ref.shape[0]
    n_split = 2 if tm >= 2 * LANES else 1
    for part in range(n_split):
        rs = slice(part * (tm // n_split), (part + 1) * (tm // n_split))
        _pre_rows(rs, x_ref, sh_ref, sc_ref, cos_ref, sin_ref, gattn_ref, gql_ref, gkv_ref, gq_ref, gkr_ref,
                  win_ref, wuq_ref, wuk_ref, wuv_ref,
                  q_ref, k_ref, v_ref, sbq_ref, sbkb_ref, sbvb_ref, ckv_ref, kpe_ref, kinv_ref,
                  sbk_ref, sbv_ref, maybe_qabs_ref)


def _pre_rows(rs, x_ref, sh_ref, sc_ref, cos_ref, sin_ref, gattn_ref, gql_ref, gkv_ref, gq_ref, gkr_ref,
              win_ref, wuq_ref, wuk_ref, wuv_ref,
              q_ref, k_ref, v_ref, sbq_ref, sbkb_ref, sbvb_ref, ckv_ref, kpe_ref, kinv_ref,
              sbk_ref, sbv_ref, maybe_qabs_ref):
    def mod(ref):
        return ref[0] if ref.shape[1] == 1 else ref[0, rs, :]

    x = x_ref[rs, :]
    h = _rms(x, gattn_ref[...]) * (1.0 + mod(sc_ref)) + mod(sh_ref)
    proj = _dot(h.astype(BF16), win_ref[...])
    o_kv = Q_LORA
    o_sbq = o_kv + KV_LORA
    o_sbk = o_sbq + SB_WIDTH
    o_sbv = o_sbk + SB_WIDTH
    o_kr = o_sbv + SB_WIDTH
    q_lat = proj[:, :o_kv]
    kv_lat = proj[:, o_kv:o_sbq]
    sbq = proj[:, o_sbq:o_sbk]
    sbk = proj[:, o_sbk:o_sbv]
    sbv = proj[:, o_sbv:o_kr]
    kr = proj[:, o_kr:o_kr + HEAD_PAD]

    cos_t = cos_ref[rs, :]
    sin_t = sin_ref[rs, :]

    sbq_ref[rs, :] = (sbq * SB_SCALE).astype(BF16)
    sbkb_ref[rs, :] = sbk.astype(BF16)
    sbvb_ref[rs, :] = sbv.astype(BF16)
    if len(sbk_ref.shape) == 3:
        sbk_ref[0, :, rs] = sbk.T
        sbv_ref[0, :, rs] = sbv.T
    else:
        sbk_ref[rs, :] = sbk
        sbv_ref[rs, :] = sbv

    q = _dot(_rms(q_lat, gql_ref[...]).astype(BF16), wuq_ref[...])
    gq = gq_ref[...]
    for hd in range(N_HEADS):
        qh = q[:, hd * HEAD_PAD:(hd + 1) * HEAD_PAD]
        inv = lax.rsqrt(jnp.sum(qh * qh, axis=-1, keepdims=True) * (1.0 / QK) + EPS)
        qh = _rope128(qh * inv * gq, cos_t, sin_t)
        qhb = qh.astype(BF16)
        q_ref[rs, hd * HEAD_PAD:(hd + 1) * HEAD_PAD] = qhb
        if maybe_qabs_ref:
            wk_h = wuk_ref[:, hd * HEAD_PAD:(hd + 1) * HEAD_PAD]
            maybe_qabs_ref[0][rs, hd * KV_LORA:(hd + 1) * KV_LORA] = _dot_nt(qhb, wk_h).astype(BF16)

    ckv = _rms(kv_lat, gkv_ref[...])
    ckv_ref[rs, :] = ckv
    cb = ckv.astype(BF16)
    kn = _dot(cb, wuk_ref[...])
    v_ref[rs, :] = _dot(cb, wuv_ref[...]).astype(BF16)
    kpe = _rope128(kr * gkr_ref[...], cos_t, sin_t)
    kpe_ref[rs, :] = kpe
    lane = lax.broadcasted_iota(I32, kpe.shape, 1)
    kinv_all = jnp.zeros_like(kpe)
    for hd in range(N_HEADS):
        knh = kn[:, hd * HEAD_PAD:(hd + 1) * HEAD_PAD]
        kp = knh + kr
        inv = lax.rsqrt(jnp.sum(kp * kp, axis=-1, keepdims=True) * (1.0 / QK) + EPS)
        k_ref[rs, hd * HEAD_PAD:(hd + 1) * HEAD_PAD] = ((knh + kpe) * (inv * (MLA_SCALE * LOG2E))).astype(BF16)
        kinv_all = jnp.where(lane == hd, inv, kinv_all)
    kinv_ref[rs, :] = kinv_all


def _pre_project(x, shift, scale, cos_t, sin_t, wts, *, tm, per_token_mod, pos_tiles, with_qabs):
    t = x.shape[0]
    n_tiles = t // tm
    if per_token_mod:
        mod_spec = pl.BlockSpec((1, tm, D_MODEL), lambda i: (i, 0, 0))
    else:
        tiles_per_seq = n_tiles // shift.shape[0]
        mod_spec = pl.BlockSpec((1, 1, D_MODEL), lambda i: (i // tiles_per_seq, 0, 0))
    tab_spec = pl.BlockSpec((tm, HEAD_PAD), lambda i: (i % pos_tiles, 0))

    def full(a):
        return pl.BlockSpec(a.shape, lambda i: (0,) * a.ndim)

    def rows(width):
        return pl.BlockSpec((tm, width), lambda i: (i, 0))

    hp = N_HEADS * HEAD_PAD
    out_shapes = [
        jax.ShapeDtypeStruct((t, hp), BF16), jax.ShapeDtypeStruct((t, hp), BF16),
        jax.ShapeDtypeStruct((t, hp), BF16),
        jax.ShapeDtypeStruct((t, SB_WIDTH), BF16), jax.ShapeDtypeStruct((t, SB_WIDTH), BF16),
        jax.ShapeDtypeStruct((t, SB_WIDTH), BF16),
        jax.ShapeDtypeStruct((t, KV_LORA), F32), jax.ShapeDtypeStruct((t, HEAD_PAD), F32),
        jax.ShapeDtypeStruct((t, HEAD_PAD), F32),
        jax.ShapeDtypeStruct((t, SB_WIDTH), F32), jax.ShapeDtypeStruct((t, SB_WIDTH), F32),
    ]
    out_specs = [rows(hp), rows(hp), rows(hp), rows(SB_WIDTH), rows(SB_WIDTH), rows(SB_WIDTH),
                 rows(KV_LORA), rows(HEAD_PAD), rows(HEAD_PAD), rows(SB_WIDTH), rows(SB_WIDTH)]
    if not per_token_mod:
        n_seq = shift.shape[0]
        kv_t = pl.BlockSpec((1, SB_WIDTH, tm), lambda i: (i // tiles_per_seq, 0, i % tiles_per_seq))
        for o in (9, 10):
            out_shapes[o] = jax.ShapeDtypeStruct((n_seq, SB_WIDTH, t // n_seq), F32)
            out_specs[o] = kv_t
    if with_qabs:
        out_shapes.append(jax.ShapeDtypeStruct((t, N_HEADS * KV_LORA), BF16))
        out_specs.append(rows(N_HEADS * KV_LORA))
    return pl.pallas_call(
        _pre_kernel,
        grid=(n_tiles,),
        in_specs=[rows(D_MODEL), mod_spec, mod_spec, tab_spec, tab_spec,
                  full(wts["g_attn"]), full(wts["g_q_lat"]), full(wts["g_kv_lat"]), full(wts["gq"]),
                  full(wts["gkr"]), full(wts["w_in"]), full(wts["w_uq"]), full(wts["w_uk"]),
                  full(wts["w_uv"])],
        out_specs=out_specs,
        out_shape=out_shapes,
        compiler_params=_params(("arbitrary",)),
        name="pre_project",
    )(x, shift, scale, cos_t, sin_t, wts["g_attn"], wts["g_q_lat"], wts["g_kv_lat"], wts["gq"],
      wts["gkr"], wts["w_in"], wts["w_uq"], wts["w_uk"], wts["w_uv"])


def _mla_prompt_kernel(q_ref, k_ref, vt_ref, o_ref, *, blk, hg):
    qi = pl.program_id(2)
    qs = [q_ref[:, g * HEAD_PAD:(g + 1) * HEAD_PAD] for g in range(hg)]

    def step(j, carry, diagonal):
        start = pl.multiple_of(j * blk, blk)
        out = []
        for g in range(hg):
            m, l, acc = carry[g]
            k = k_ref[pl.ds(start, blk), g * HEAD_PAD:(g + 1) * HEAD_PAD]
            vt = vt_ref[j, g * HEAD_PAD:(g + 1) * HEAD_PAD, :]
            s = _dot_nt(k, qs[g])
            if diagonal:
                key = lax.broadcasted_iota(I32, s.shape, 0)
                qry = lax.broadcasted_iota(I32, s.shape, 1)
                s = jnp.where(key <= qry, s, -jnp.inf)
            m_new = jnp.maximum(m, jnp.max(s, axis=0, keepdims=True))
            alpha = jnp.exp2(m - m_new)
            p = jnp.exp2(s - m_new)
            l = alpha * l + jnp.sum(p, axis=0, keepdims=True)
            acc = alpha * acc + _dot(vt, p.astype(BF16))
            out.append((m_new, l, acc))
        return tuple(out)

    init = tuple((jnp.full((1, blk), -jnp.inf, F32), jnp.zeros((1, blk), F32),
                  jnp.zeros((HEAD_PAD, blk), F32)) for _ in range(hg))
    carry = lax.fori_loop(0, qi, lambda j, c: step(j, c, False), init)
    carry = step(qi, carry, True)
    for g in range(hg):
        _, l, acc = carry[g]
        o_ref[0, g * HEAD_PAD:(g + 1) * HEAD_PAD, :] = (acc / l).astype(BF16)


def _mla_prompt(q, k, v, batch, seq, blk, hg):
    nq = seq // blk
    width = hg * HEAD_PAD
    hp = N_HEADS * HEAD_PAD
    vt = v.reshape(batch * nq, blk, hp).transpose(0, 2, 1)
    ot = pl.pallas_call(
        functools.partial(_mla_prompt_kernel, blk=blk, hg=hg),
        grid=(batch, N_HEADS // hg, nq),
        in_specs=[pl.BlockSpec((blk, width), lambda b, h, i: (b * nq + i, h)),
                  pl.BlockSpec((seq, width), lambda b, h, i: (b, h)),
                  pl.BlockSpec((nq, width, blk), lambda b, h, i: (b, h, 0))],
        out_specs=pl.BlockSpec((1, width, blk), lambda b, h, i: (b * nq + i, h, 0)),
        out_shape=jax.ShapeDtypeStruct((batch * nq, hp, blk), BF16),
        compiler_params=_params(("arbitrary", "arbitrary", "arbitrary")),
        name="mla_prompt",
    )(q, k, vt)
    return ot.transpose(0, 2, 1).reshape(batch * seq, hp)


def _sb_prompt_kernel(q_ref, k_ref, v_ref, o_ref, *, blk):
    qi = pl.program_id(2)
    q2 = q_ref[...]
    lane = lax.broadcasted_iota(I32, q2.shape, 1)
    row = lax.broadcasted_iota(I32, (blk, blk), 0)
    col = lax.broadcasted_iota(I32, (blk, blk), 1)
    later_mat = (row > col).astype(BF16)
    zero = jnp.zeros_like(q2)
    qms = [jnp.where(lane < SB_DIM, q2, zero), jnp.where(lane >= SB_DIM, q2, zero)]

    def cond(c):
        return jnp.logical_and(c[0] >= 0, c[1] == 0)

    def body(c):
        j, _, tails, accs = c
        start = pl.multiple_of(j * blk, blk)
        k = k_ref[pl.ds(start, blk), :]
        v = v_ref[pl.ds(start, blk), :]
        mask = (col + j * blk) < (row + qi * blk)
        new_tails, new_accs = [], []
        for hh in range(2):
            z = _dot_nt(qms[hh], k)
            sp = _softplus(z)
            lf = jnp.where(mask, -sp, 0.0)
            hi, lo = _split_bf16(lf)
            later = _dot(hi, later_mat) + _dot(lo, later_mat)
            a = jnp.where(mask, jnp.exp((z - sp) + later + tails[hh]), 0.0)
            new_accs.append(accs[hh] + _dot(a.astype(BF16), v))
            new_tails.append(tails[hh] + jnp.sum(lf, axis=1, keepdims=True))
        most = jnp.max(jnp.maximum(new_tails[0], new_tails[1]))
        return j - 1, (most < SB_DONE).astype(I32), tuple(new_tails), tuple(new_accs)

    zt = jnp.zeros((blk, 1), F32)
    za = jnp.zeros((blk, 2 * SB_DIM), F32)
    accs = lax.while_loop(cond, body, (qi, jnp.int32(0), (zt, zt), (za, za)))[3]
    o_ref[...] = jnp.where(lane < SB_DIM, accs[0], accs[1]).astype(BF16)


def _sb_prompt(q, k, v, batch, seq, blk):
    nq = seq // blk
    pairs = SB_WIDTH // LANES
    return pl.pallas_call(
        functools.partial(_sb_prompt_kernel, blk=blk),
        grid=(batch, pairs, nq),
        in_specs=[pl.BlockSpec((blk, LANES), lambda b, h, i: (b * nq + i, h)),
                  pl.BlockSpec((seq, LANES), lambda b, h, i: (b, h)),
                  pl.BlockSpec((seq, LANES), lambda b, h, i: (b, h))],
        out_specs=pl.BlockSpec((blk, LANES), lambda b, h, i: (b * nq + i, h)),
        out_shape=jax.ShapeDtypeStruct(q.shape, BF16),
        compiler_params=_params(("arbitrary", "arbitrary", "arbitrary")),
        name="sb_prompt",
    )(q, k, v)


def _mla_decode_kernel(pt_ref, qabs_ref, qrope_ref, cnew_ref, penew_ref, invnew_ref,
                       ckv_hbm, kpe_hbm, kinv_hbm, o_ref, ckv_buf, kpe_buf, kinv_buf, sem,
                       *, layer, n_pg, n_chunks):
    b = pl.program_id(0)
    total = pl.num_programs(0) * n_chunks
    qa = qabs_ref[0]
    qr = qrope_ref[0]

    def copies(seq, chunk, slot):
        out = []
        for p in range(n_pg):
            page = pt_ref[seq, chunk * n_pg + p]
            out.append(pltpu.make_async_copy(ckv_hbm.at[layer, page], ckv_buf.at[slot, p], sem.at[slot, 0]))
            out.append(pltpu.make_async_copy(kpe_hbm.at[layer, page], kpe_buf.at[slot, p], sem.at[slot, 1]))
            out.append(pltpu.make_async_copy(kinv_hbm.at[layer, page], kinv_buf.at[slot, p], sem.at[slot, 2]))
        return out

    def start(g):
        @pl.when(g < total)
        def _():
            for c in copies(g // n_chunks, g % n_chunks, g % DECODE_SLOTS):
                c.start()

    def wait(slot):
        for c in copies(0, 0, slot):
            c.wait()

    @pl.when(b == 0)
    def _():
        for g in range(DECODE_DEPTH):
            start(jnp.int32(g))

    c_new = cnew_ref[0]
    s0 = (jnp.sum(qa.astype(F32)[:N_HEADS] * c_new, axis=1, keepdims=True)
          + jnp.sum(qr.astype(F32)[:N_HEADS] * penew_ref[0], axis=1, keepdims=True))
    init = (s0 * invnew_ref[0] * MLA_SCALE, jnp.ones((N_HEADS, 1), F32),
            jnp.broadcast_to(c_new, (N_HEADS, KV_LORA)))

    def chunk_step(slot, carry):
        m_old, l, acc = carry
        cb = ckv_buf[slot].reshape(n_pg * PAGE, KV_LORA).astype(BF16)
        kpe_t = jnp.concatenate([kpe_buf[slot, p] for p in range(n_pg)], axis=1)
        kinv_t = jnp.concatenate([kinv_buf[slot, p] for p in range(n_pg)], axis=1)
        s = (_dot_nt(qa, cb) + _dot(qr, kpe_t.astype(BF16)))[:N_HEADS]
        s = s * kinv_t * MLA_SCALE
        m_new = jnp.maximum(m_old, jnp.max(s, axis=1, keepdims=True))
        alpha = jnp.exp(m_old - m_new)
        p = jnp.exp(s - m_new)
        l = alpha * l + jnp.sum(p, axis=1, keepdims=True)
        p16 = jnp.concatenate([p, jnp.zeros_like(p)], axis=0).astype(BF16)
        acc = alpha * acc + _dot(p16, cb)[:N_HEADS]
        return m_new, l, acc

    def chunk(c, carry):
        g = b * n_chunks + c
        start(g + DECODE_DEPTH)
        slot = g % DECODE_SLOTS
        wait(slot)
        return chunk_step(slot, carry)

    _, l, acc = lax.fori_loop(0, n_chunks, chunk, init)
    o_ref[0] = acc / l


def _mla_decode(layer, page_table, qabs, qrope, c_new, pe_new, inv_new, cache_ckv, cache_kpe_t, cache_kinv_t,
                n_pg):
    nb, n_pages = page_table.shape
    assert n_pages % n_pg == 0
    hr = 2 * N_HEADS
    anyspec = pl.BlockSpec(memory_space=pl.ANY)
    return pl.pallas_call(
        functools.partial(_mla_decode_kernel, layer=layer, n_pg=n_pg, n_chunks=n_pages // n_pg),
        grid_spec=pltpu.PrefetchScalarGridSpec(
            num_scalar_prefetch=1,
            grid=(nb,),
            in_specs=[pl.BlockSpec((1, hr, KV_LORA), lambda b, pt: (b, 0, 0)),
                      pl.BlockSpec((1, hr, ROPE), lambda b, pt: (b, 0, 0)),
                      pl.BlockSpec((1, 1, KV_LORA), lambda b, pt: (b, 0, 0)),
                      pl.BlockSpec((1, 1, ROPE), lambda b, pt: (b, 0, 0)),
                      pl.BlockSpec((1, N_HEADS, 1), lambda b, pt: (b, 0, 0)),
                      anyspec, anyspec, anyspec],
            out_specs=pl.BlockSpec((1, N_HEADS, KV_LORA), lambda b, pt: (b, 0, 0)),
            scratch_shapes=[pltpu.VMEM((DECODE_SLOTS, n_pg, PAGE, KV_LORA), F32),
                            pltpu.VMEM((DECODE_SLOTS, n_pg, ROPE, PAGE), F32),
                            pltpu.VMEM((DECODE_SLOTS, n_pg, N_HEADS, PAGE), F32),
                            pltpu.SemaphoreType.DMA((DECODE_SLOTS, 3))]),
        out_shape=jax.ShapeDtypeStruct((nb, N_HEADS, KV_LORA), F32),
        compiler_params=_params(("arbitrary",)),
        name="mla_decode",
    )(page_table, qabs, qrope, c_new, pe_new, inv_new, cache_ckv, cache_kpe_t, cache_kinv_t)


def _sb_page(q_ref, kt_ref, vt_ref, tail, accs):
    head = lax.broadcasted_iota(I32, (N_HEADS, PAGE), 0)
    lane = lax.broadcasted_iota(I32, (N_HEADS, PAGE), 1)
    z = jnp.zeros((N_HEADS, PAGE), F32)
    for h in range(N_HEADS):
        zh = jnp.sum(kt_ref[h] * q_ref[h], axis=0, keepdims=True)
        z = jnp.where(head == h, zh, z)
    sp = _softplus(z)
    lf = -sp
    suf = lf
    shift = 1
    while shift < PAGE:
        moved = pltpu.roll(suf, PAGE - shift, 1)
        suf = suf + jnp.where(lane < PAGE - shift, moved, 0.0)
        shift *= 2
    a = jnp.exp((z - sp) + (suf - lf) + tail)
    new_accs = []
    for h in range(N_HEADS):
        new_accs.append(accs[h] + jnp.sum(vt_ref[h] * a[h:h + 1, :], axis=1, keepdims=True))
    return tail + suf[:, 0:1], new_accs


def _sb_decode_kernel(pt_ref, q_ref, k0_ref, v0_ref, k1_ref, v1_ref, kc_ref, vc_ref, o_ref,
                      kbuf, vbuf, sem, *, layer, n_pages, n_pre):
    b = pl.program_id(0)
    q = q_ref.at[0]
    tail = jnp.zeros((N_HEADS, 1), F32)
    accs = [jnp.zeros((SB_DIM, 1), F32) for _ in range(N_HEADS)]
    tail, accs = _sb_page(q, k0_ref.at[0, 0], v0_ref.at[0, 0], tail, accs)
    if n_pre > 1:
        tail, accs = _sb_page(q, k1_ref.at[0, 0], v1_ref.at[0, 0], tail, accs)

    def cond(c):
        return jnp.logical_and(c[0] >= 0, c[1] == 0)

    def body(c):
        j, _, tail, accs = c
        page = pt_ref[b, j]
        ck = pltpu.make_async_copy(kc_ref.at[layer, page], kbuf, sem.at[0])
        cv = pltpu.make_async_copy(vc_ref.at[layer, page], vbuf, sem.at[1])
        ck.start()
        cv.start()
        ck.wait()
        cv.wait()
        tail, accs = _sb_page(q, kbuf, vbuf, tail, list(accs))
        return j - 1, (jnp.max(tail) < SB_DONE).astype(I32), tail, tuple(accs)

    done0 = (jnp.max(tail) < SB_DONE).astype(I32)
    accs = lax.while_loop(cond, body, (jnp.int32(n_pages - 1 - n_pre), done0, tail, tuple(accs)))[3]
    for h in range(N_HEADS):
        o_ref[0, h] = accs[h]


def _sb_decode(layer, page_table, q, cache_kt, cache_vt):
    nb, n_pages = page_table.shape
    n_pre = 2 if n_pages >= 2 else 1
    page_shape = (N_HEADS, SB_DIM, PAGE)

    def pg(back):
        return pl.BlockSpec((1, 1) + page_shape, lambda b, pt: (layer, pt[b, n_pages - 1 - back], 0, 0, 0))

    qspec = pl.BlockSpec((1, N_HEADS, SB_DIM, 1), lambda b, pt: (b, 0, 0, 0))
    return pl.pallas_call(
        functools.partial(_sb_decode_kernel, layer=layer, n_pages=n_pages, n_pre=n_pre),
        grid_spec=pltpu.PrefetchScalarGridSpec(
            num_scalar_prefetch=1,
            grid=(nb,),
            in_specs=[qspec, pg(0), pg(0), pg(n_pre - 1), pg(n_pre - 1),
                      pl.BlockSpec(memory_space=pl.ANY), pl.BlockSpec(memory_space=pl.ANY)],
            out_specs=qspec,
            scratch_shapes=[pltpu.VMEM(page_shape, F32), pltpu.VMEM(page_shape, F32),
                            pltpu.SemaphoreType.DMA((2,))]),
        out_shape=jax.ShapeDtypeStruct((nb, N_HEADS, SB_DIM, 1), F32),
        compiler_params=_params(("arbitrary",)),
        name="sb_decode",
    )(page_table, q, cache_kt, cache_vt, cache_kt, cache_vt, cache_kt, cache_vt)


def _uv_kernel(o_ref, w_ref, out_ref):
    for hd in range(N_HEADS):
        out_ref[:, hd * HEAD_PAD:(hd + 1) * HEAD_PAD] = _dot(
            o_ref[:, hd * KV_LORA:(hd + 1) * KV_LORA].astype(BF16), w_ref[:, hd * HEAD_PAD:(hd + 1) * HEAD_PAD]
        ).astype(BF16)


def _value_up(o_lat, w_uv_p):
    t = o_lat.shape[0]
    return pl.pallas_call(
        _uv_kernel,
        out_shape=jax.ShapeDtypeStruct((t, N_HEADS * HEAD_PAD), BF16),
        compiler_params=pltpu.CompilerParams(vmem_limit_bytes=VMEM_LIMIT),
        name="value_up",
    )(o_lat, w_uv_p)


def _post_kernel(x_ref, om_ref, os_ref, ga_ref, shm_ref, scm_ref, gm_ref, gmoe_ref, wom_ref, wos_ref,
                 wgu_ref, wd_ref, base_ref, h2_ref):
    attn = _dot(om_ref[...], wom_ref[...]) + _dot(os_ref[...], wos_ref[...])
    xm = x_ref[...] + ga_ref[0] * attn
    h2 = _rms(xm, gmoe_ref[...]) * (1.0 + scm_ref[0]) + shm_ref[0]
    h2_ref[...] = h2
    gu = _dot(h2.astype(BF16), wgu_ref[...])
    g = gu[:, :D_SHARED]
    act = (g * _sigmoid(g)) * gu[:, D_SHARED:]
    ysh = _dot(act.astype(BF16), wd_ref[...])
    base_ref[...] = xm + gm_ref[0] * ysh


def _post_attention(x, o_mla, o_sb, mods, wts, *, tm, per_token_mod):
    t = x.shape[0]
    n_tiles = t // tm
    g_a, sh_m, sc_m, g_m = mods
    if per_token_mod:
        mod_spec = pl.BlockSpec((1, tm, D_MODEL), lambda i: (i, 0, 0))
    else:
        tiles_per_seq = n_tiles // g_a.shape[0]
        mod_spec = pl.BlockSpec((1, 1, D_MODEL), lambda i: (i // tiles_per_seq, 0, 0))

    def full(a):
        return pl.BlockSpec(a.shape, lambda i: (0,) * a.ndim)

    def rows(width):
        return pl.BlockSpec((tm, width), lambda i: (i, 0))

    buf = jax.ShapeDtypeStruct((t, D_MODEL), F32)
    return pl.pallas_call(
        _post_kernel,
        grid=(n_tiles,),
        in_specs=[rows(D_MODEL), rows(N_HEADS * HEAD_PAD), rows(SB_WIDTH), mod_spec, mod_spec, mod_spec,
                  mod_spec, full(wts["g_moe"]), full(wts["w_o_mla"]), full(wts["w_o_sb"]),
                  full(wts["w_sh_gu"]), full(wts["w_sh_down"])],
        out_specs=[rows(D_MODEL), rows(D_MODEL)],
        out_shape=[buf, buf],
        compiler_params=_params(("arbitrary",)),
        name="post_attention",
    )(x, o_mla, o_sb, g_a, sh_m, sc_m, g_m, wts["g_moe"], wts["w_o_mla"], wts["w_o_sb"],
      wts["w_sh_gu"], wts["w_sh_down"])


def _router_kernel(hp_ref, hs_ref, whi_ref, wlo_ref, b_ref, idx_ref, wgt_ref, rank_ref, cnt_ref, *,
                   n_prompt_tiles):
    i = pl.program_id(0)

    @pl.when(i == 0)
    def _():
        cnt_ref[...] = jnp.zeros_like(cnt_ref)

    h = jnp.where(i < n_prompt_tiles, hp_ref[...], hs_ref[...])
    tm = h.shape[0]
    hhi, hlo = _split_bf16(h)
    whi = whi_ref[...]
    logits = _dot_nt(whi, hhi) + _dot_nt(whi, hlo) + _dot_nt(wlo_ref[...], hhi)
    scores = _sigmoid(logits)
    sel = scores + b_ref[...]
    neg = -jnp.inf

    blocks = [sel[g * GROUP_SIZE:(g + 1) * GROUP_SIZE, :] for g in range(N_GROUPS)]
    gs = []
    for blk in blocks:
        m1 = jnp.max(blk, axis=0, keepdims=True)
        is_max = blk == m1
        n_max = jnp.sum(is_max.astype(F32), axis=0, keepdims=True)
        m2 = jnp.max(jnp.where(is_max, neg, blk), axis=0, keepdims=True)
        gs.append(m1 + jnp.where(n_max >= 2.0, m1, m2))
    kept = []
    for g in range(N_GROUPS):
        ahead = jnp.zeros_like(gs[g])
        for g2 in range(N_GROUPS):
            if g2 == g:
                continue
            better = gs[g2] > gs[g]
            if g2 < g:
                better = jnp.logical_or(better, gs[g2] == gs[g])
            ahead = ahead + better.astype(F32)
        kept.append(jnp.where(ahead < float(TOPK_GROUPS), blocks[g], neg))
    cur = jnp.concatenate(kept, axis=0)

    rowf = lax.broadcasted_iota(I32, cur.shape, 0).astype(F32)
    chosen = jnp.zeros_like(cur)
    idxs, ws = [], []
    for _ in range(TOP_K):
        m = jnp.max(cur, axis=0, keepdims=True)
        ik = jnp.min(jnp.where(cur == m, rowf, float(N_EXPERTS)), axis=0, keepdims=True)
        hit = rowf == ik
        ws.append(jnp.sum(jnp.where(hit, scores, 0.0), axis=0, keepdims=True))
        cur = jnp.where(hit, neg, cur)
        chosen = chosen + hit.astype(F32)
        idxs.append(ik)
    wsum = ws[0]
    for w in ws[1:]:
        wsum = wsum + w
    ws = [w / wsum * ROUTE_SCALE for w in ws]

    r2 = lax.broadcasted_iota(I32, (tm, tm), 0)
    c2 = lax.broadcasted_iota(I32, (tm, tm), 1)
    left = (r2 < c2).astype(BF16)
    pos = _dot(chosen.astype(BF16), left) + cnt_ref[...]
    ranks = [jnp.sum(jnp.where(rowf == ik, pos, 0.0), axis=0, keepdims=True) for ik in idxs]
    cnt_ref[...] = cnt_ref[...] + jnp.sum(chosen, axis=1, keepdims=True)

    idx_ref[...] = jnp.concatenate(idxs, axis=0).astype(I32)
    wgt_ref[...] = jnp.concatenate(ws, axis=0)
    rank_ref[...] = jnp.concatenate(ranks, axis=0).astype(I32)


def _route(h2_p, h2_s, whi, wlo, b_col, tm):
    n_p = h2_p.shape[0] // tm
    t = h2_p.shape[0] + h2_s.shape[0]
    sm = pl.BlockSpec((TOP_K, tm), lambda i: (0, i))
    return pl.pallas_call(
        functools.partial(_router_kernel, n_prompt_tiles=n_p),
        grid=(t // tm,),
        in_specs=[pl.BlockSpec((tm, D_MODEL), lambda i: (jnp.minimum(i, n_p - 1), 0)),
                  pl.BlockSpec((tm, D_MODEL), lambda i: (jnp.maximum(i - n_p, 0), 0)),
                  pl.BlockSpec(whi.shape, lambda i: (0, 0)),
                  pl.BlockSpec(wlo.shape, lambda i: (0, 0)),
                  pl.BlockSpec(b_col.shape, lambda i: (0, 0))],
        out_specs=[sm, sm, sm, pl.BlockSpec((N_EXPERTS, 1), lambda i: (0, 0))],
        out_shape=[jax.ShapeDtypeStruct((TOP_K, t), I32), jax.ShapeDtypeStruct((TOP_K, t), F32),
                   jax.ShapeDtypeStruct((TOP_K, t), I32), jax.ShapeDtypeStruct((N_EXPERTS, 1), F32)],
        compiler_params=_params(("arbitrary",)),
        name="router",
    )(h2_p, h2_s, whi, wlo, b_col)


def _row(ref, r):
    return ref.at[pl.ds(r, 1), :]


def _dispatch_kernel(dest_ref, hp_ref, hs_ref, xs_ref, sem, *, tm, n_prompt_tiles):
    i = pl.program_id(0)

    def scatter_rows(src_ref):
        def issue(t2, carry):
            for t in (2 * t2, 2 * t2 + 1):
                for k in range(TOP_K):
                    pltpu.make_async_copy(_row(src_ref, t), _row(xs_ref, dest_ref[k, t]), sem).start(priority=k % 2)
            return carry

        lax.fori_loop(0, tm // 2, issue, 0)

        def drain(t, carry):
            for _ in range(TOP_K):
                pltpu.make_async_copy(_row(src_ref, 0), _row(xs_ref, 0), sem).wait()
            return carry

        lax.fori_loop(0, tm, drain, 0)

    @pl.when(i < n_prompt_tiles)
    def _():
        scatter_rows(hp_ref)

    @pl.when(i >= n_prompt_tiles)
    def _():
        scatter_rows(hs_ref)


def _dispatch(dest, h2_p, h2_s, n_rows, tm):
    n_p = h2_p.shape[0] // tm
    t = h2_p.shape[0] + h2_s.shape[0]
    return pl.pallas_call(
        functools.partial(_dispatch_kernel, tm=tm, n_prompt_tiles=n_p),
        grid=(t // tm,),
        in_specs=[pl.BlockSpec((TOP_K, tm), lambda i: (0, i), memory_space=pltpu.SMEM),
                  pl.BlockSpec((tm, D_MODEL), lambda i: (jnp.minimum(i, n_p - 1), 0)),
                  pl.BlockSpec((tm, D_MODEL), lambda i: (jnp.maximum(i - n_p, 0), 0))],
        out_specs=pl.BlockSpec(memory_space=pl.ANY),
        scratch_shapes=[pltpu.SemaphoreType.DMA(())],
        out_shape=jax.ShapeDtypeStruct((n_rows, D_MODEL), F32),
        compiler_params=_params(("arbitrary",)),
        name="dispatch",
    )(dest, h2_p, h2_s)


def _expert_kernel(be_ref, nb_ref, xs_ref, wg_ref, wu_ref, wd_ref, ys_ref, wgu_s, wd_s):
    i = pl.program_id(0)

    @pl.when(i < nb_ref[0])
    def _():
        prev = be_ref[jnp.maximum(i - 1, 0)]

        @pl.when(jnp.logical_or(i == 0, be_ref[i] != prev))
        def _():
            wgu_s[:, :D_EXPERT] = wg_ref[0].astype(BF16)
            wgu_s[:, D_EXPERT:] = wu_ref[0].astype(BF16)
            wd_s[...] = wd_ref[0].astype(BF16)

        gu = _dot(xs_ref[...].astype(BF16), wgu_s[...])
        g = gu[:, :D_EXPERT]
        act = (g * _sigmoid(g)) * gu[:, D_EXPERT:]
        ys_ref[...] = _dot(act.astype(BF16), wd_s[...])


def _experts(layer, block_e, n_used, xs, w_gate, w_up, w_down):
    n_rows = xs.shape[0]
    n_blocks = n_rows // EXPERT_BLOCK
    w_gate, w_up, w_down = (w.reshape((-1,) + w.shape[2:]) for w in (w_gate, w_up, w_down))

    def row_map(i, be, nb):
        return (jnp.minimum(i, nb[0] - 1), 0)

    def w_map(i, be, nb):
        return (layer * N_EXPERTS + be[jnp.minimum(i, nb[0] - 1)], 0, 0)

    return pl.pallas_call(
        _expert_kernel,
        grid_spec=pltpu.PrefetchScalarGridSpec(
            num_scalar_prefetch=2,
            grid=(n_blocks,),
            in_specs=[pl.BlockSpec((EXPERT_BLOCK, D_MODEL), row_map),
                      pl.BlockSpec((1, D_MODEL, D_EXPERT), w_map),
                      pl.BlockSpec((1, D_MODEL, D_EXPERT), w_map),
                      pl.BlockSpec((1, D_EXPERT, D_MODEL), w_map)],
            out_specs=pl.BlockSpec((EXPERT_BLOCK, D_MODEL), row_map),
            scratch_shapes=[pltpu.VMEM((D_MODEL, 2 * D_EXPERT), BF16), pltpu.VMEM((D_EXPERT, D_MODEL), BF16)]),
        out_shape=jax.ShapeDtypeStruct(xs.shape, F32),
        compiler_params=_params(("arbitrary",)),
        name="experts",
    )(block_e, n_used, xs, w_gate, w_up, w_down)


def _combine_kernel(dest_ref, dest_next_ref, wgt_ref, ys_ref, basep_ref, bases_ref, gmp_ref, gms_ref,
                    outp_ref, outs_ref, buf, sem, *, tm, n_prompt_tiles):
    i = pl.program_id(0)
    slot = i % 2

    def gather_rows(rows_ref, into):
        def issue(t2, carry):
            for t in (2 * t2, 2 * t2 + 1):
                for k in range(TOP_K):
                    pltpu.make_async_copy(_row(ys_ref, rows_ref[k, t]), _row(buf.at[into, k], t),
                                          sem.at[into]).start(priority=k % 2)
            return carry

        lax.fori_loop(0, tm // 2, issue, 0)

    @pl.when(i == 0)
    def _():
        gather_rows(dest_ref, 0)

    more = i + 1 < pl.num_programs(0)

    @pl.when(jnp.logical_and(more, slot == 0))
    def _():
        gather_rows(dest_next_ref, 1)

    @pl.when(jnp.logical_and(more, slot == 1))
    def _():
        gather_rows(dest_next_ref, 0)

    def drain(t, carry):
        for k in range(TOP_K):
            pltpu.make_async_copy(_row(ys_ref, 0), _row(buf.at[slot, k], 0), sem.at[slot]).wait()
        return carry

    lax.fori_loop(0, tm, drain, 0)

    w = wgt_ref[...]
    mix = w[:, 0:1] * buf[slot, 0]
    for k in range(1, TOP_K):
        mix = mix + w[:, k:k + 1] * buf[slot, k]

    @pl.when(i < n_prompt_tiles)
    def _():
        outp_ref[...] = basep_ref[...] + gmp_ref[0] * mix

    @pl.when(i >= n_prompt_tiles)
    def _():
        outs_ref[...] = bases_ref[...] + gms_ref[...] * mix


def _combine(dest, wgt_t, ys, base_p, base_s, gm_prompt, gm_sample, tm, seq):
    t_all = base_p.shape[0] + base_s.shape[0]
    n_batch = gm_prompt.shape[0]
    n_prompt_tiles = n_batch * seq // tm
    tiles_per_seq = seq // tm
    tile = (tm, D_MODEL)
    prompt_tile = pl.BlockSpec(tile, lambda i: (jnp.minimum(i, n_prompt_tiles - 1), 0))
    sample_tile = pl.BlockSpec(tile, lambda i: (jnp.maximum(i - n_prompt_tiles, 0), 0))
    n_tiles = t_all // tm
    return pl.pallas_call(
        functools.partial(_combine_kernel, tm=tm, n_prompt_tiles=n_prompt_tiles),
        grid=(n_tiles,),
        in_specs=[pl.BlockSpec((TOP_K, tm), lambda i: (0, i), memory_space=pltpu.SMEM),
                  pl.BlockSpec((TOP_K, tm), lambda i: (0, jnp.minimum(i + 1, n_tiles - 1)), memory_space=pltpu.SMEM),
                  pl.BlockSpec((tm, TOP_K), lambda i: (i, 0)),
                  pl.BlockSpec(memory_space=pl.ANY), prompt_tile, sample_tile,
                  pl.BlockSpec((1, 1, D_MODEL), lambda i: (jnp.minimum(i // tiles_per_seq, n_batch - 1), 0, 0)),
                  sample_tile],
        out_specs=[prompt_tile, sample_tile],
        scratch_shapes=[pltpu.VMEM((2, TOP_K, tm, D_MODEL), F32), pltpu.SemaphoreType.DMA((2,))],
        out_shape=[jax.ShapeDtypeStruct(base_p.shape, F32), jax.ShapeDtypeStruct(base_s.shape, F32)],
        compiler_params=_params(("arbitrary",)),
        name="combine",
    )(dest, dest, wgt_t, ys, base_p, base_s, gm_prompt, gm_sample)


def _pad_heads(w, width):
    k = w.shape[0]
    w = w.reshape(k, N_HEADS, width)
    return jnp.pad(w, ((0, 0), (0, 0), (0, HEAD_PAD - width))).reshape(k, N_HEADS * HEAD_PAD)


def _rope_tables(pos):
    half = ROPE // 2
    inv_freq = ROPE_THETA ** (-jnp.arange(half, dtype=F32) * (2.0 / ROPE))
    ang = pos.astype(F32)[:, None] * inv_freq[None, :]
    cos, sin = jnp.cos(ang), jnp.sin(ang)
    n = pos.shape[0]
    ones = jnp.ones((n, NOPE), F32)
    zeros = jnp.zeros((n, HEAD_PAD - QK), F32)
    cos_t = jnp.concatenate([ones, cos, cos, zeros], axis=1)
    sin_t = jnp.concatenate([jnp.zeros((n, NOPE), F32), -sin, sin, zeros], axis=1)
    return cos_t, sin_t


def _layer_weights(l, w_ada, b_ada, g_attn, g_moe, w_in, g_q_lat, w_uq, g_kv_lat, w_uk, w_uv, g_qk_q, g_qk_k,
                   w_o, w_router, b_router, w_sh_gate, w_sh_up, w_sh_down):
    wi = w_in[l]
    o1 = Q_LORA + KV_LORA
    o2 = o1 + ROPE
    kr_block = jnp.zeros((D_MODEL, HEAD_PAD), F32).at[:, NOPE:QK].set(wi[:, o1:o2])
    w_in_p = jnp.concatenate([wi[:, :o1], wi[:, o2:], kr_block], axis=1).astype(BF16)
    gq = jnp.concatenate([g_qk_q[l][:NOPE] * g_qk_k[l][:NOPE], g_qk_q[l][NOPE:],
                          jnp.zeros((HEAD_PAD - QK,), F32)]).reshape(1, HEAD_PAD)
    gkr = jnp.concatenate([jnp.zeros((NOPE,), F32), g_qk_k[l][NOPE:],
                           jnp.zeros((HEAD_PAD - QK,), F32)]).reshape(1, HEAD_PAD)
    wo = w_o[l]
    n_mla = N_HEADS * NOPE
    w_o_mla = jnp.pad(wo[:n_mla].reshape(N_HEADS, NOPE, D_MODEL),
                      ((0, 0), (0, HEAD_PAD - NOPE), (0, 0))).reshape(N_HEADS * HEAD_PAD, D_MODEL)
    wr_t = w_router[l].T
    whi = wr_t.astype(BF16)
    wlo = (wr_t - whi.astype(F32)).astype(BF16)
    return dict(
        w_ada=w_ada[l], b_ada=b_ada[l],
        g_attn=g_attn[l].reshape(1, -1), g_moe=g_moe[l].reshape(1, -1),
        g_q_lat=g_q_lat[l].reshape(1, -1), g_kv_lat=g_kv_lat[l].reshape(1, -1), gq=gq, gkr=gkr,
        w_in=w_in_p, w_uq=_pad_heads(w_uq[l], QK).astype(BF16),
        w_uk=_pad_heads(w_uk[l], NOPE).astype(BF16), w_uv=_pad_heads(w_uv[l], NOPE).astype(BF16),
        w_o_mla=w_o_mla.astype(BF16), w_o_sb=wo[n_mla:].astype(BF16),
        w_r_hi=whi, w_r_lo=wlo, b_router=b_router[l].reshape(-1, 1),
        w_sh_gu=jnp.concatenate([w_sh_gate[l], w_sh_up[l]], axis=1).astype(BF16),
        w_sh_down=w_sh_down[l].astype(BF16),
    )


def _tile(n, pref):
    t = min(n, pref)
    while n % t:
        t //= 2
    return t


def _layer(l, xp, xs_, cache_mla_ckv, cache_mla_kpe, cache_mla_kinv, cache_sb_k, cache_sb_v, page_table,
           c_prompt, c_sample, wts, w_exp_gate, w_exp_up, w_exp_down):
    batch, seq, _ = xp.shape
    nb = xs_.shape[0]
    n_pages = page_table.shape[1]
    past = n_pages * PAGE
    t_p = batch * seq
    tok_tile = LANES
    assert xs_.shape[1] == 1 and nb % tok_tile == 0 and seq % tok_tile == 0
    t_s = nb
    tm_s = tok_tile
    t_all = t_p + t_s

    mods = _adaln(jnp.concatenate([c_prompt, c_sample], axis=0), wts["w_ada"], wts["b_ada"])
    mods_p = [m.reshape(batch, 1, D_MODEL) for m in jnp.split(mods[:batch], 6, axis=-1)]
    mods_s = [m.reshape(t_s // tm_s, tm_s, D_MODEL) for m in jnp.split(mods[batch:], 6, axis=-1)]

    tm_p = _tile(seq, 512)
    cos_p, sin_p = _rope_tables(jnp.arange(seq))
    (q, k, v, sbq, sbkb, sbvb, ckv_p, kpe_p, kinv_p, sbk_p, sbv_p) = _pre_project(
        xp.reshape(t_p, D_MODEL), mods_p[0], mods_p[1], cos_p, sin_p, wts,
        tm=tm_p, per_token_mod=False, pos_tiles=seq // tm_p, with_qabs=False)
    o_mla_p = _mla_prompt(q, k, v, batch, seq, _tile(seq, 512), 4)
    o_sb_p = _sb_prompt(sbq, sbkb, sbvb, batch, seq, _tile(seq, 256))

    cos_s, sin_s = _rope_tables(jnp.full((tm_s,), past, I32))
    (q_s, _, _, sbq_s, _, _, ckv_s, kpe_s, kinv_s, sbk_s, sbv_s, qabs_s) = _pre_project(
        xs_.reshape(t_s, D_MODEL), mods_s[0], mods_s[1], cos_s, sin_s, wts,
        tm=tm_s, per_token_mod=True, pos_tiles=1, with_qabs=True)
    head_rows = N_HEADS
    qabs3 = jnp.pad(qabs_s.reshape(t_s, N_HEADS, KV_LORA), ((0, 0), (0, head_rows), (0, 0)))
    qrope3 = jnp.pad(q_s.reshape(t_s, N_HEADS, HEAD_PAD)[:, :, NOPE:QK], ((0, 0), (0, head_rows), (0, 0)))
    n_pg = _tile(n_pages, 16)
    o_lat = _mla_decode(l, page_table, qabs3, qrope3, ckv_s.reshape(t_s, 1, KV_LORA),
                        kpe_s[:, NOPE:QK].reshape(t_s, 1, ROPE), kinv_s[:, :N_HEADS].reshape(t_s, N_HEADS, 1),
                        cache_mla_ckv, jnp.swapaxes(cache_mla_kpe, 2, 3), jnp.swapaxes(cache_mla_kinv, 2, 3),
                        n_pg)
    o_mla_s = _value_up(o_lat.reshape(t_s, N_HEADS * KV_LORA), wts["w_uv"])
    o_sb_s = _sb_decode(l, page_table, sbq_s.astype(F32).reshape(t_s, N_HEADS, SB_DIM, 1),
                        cache_sb_k.transpose(0, 1, 3, 4, 2), cache_sb_v.transpose(0, 1, 3, 4, 2))
    o_sb_s = o_sb_s.reshape(t_s, SB_WIDTH).astype(BF16)

    base_p, h2_p = _post_attention(xp.reshape(t_p, D_MODEL), o_mla_p, o_sb_p, mods_p[2:], wts,
                                   tm=tm_p, per_token_mod=False)
    base_s, h2_s = _post_attention(xs_.reshape(t_s, D_MODEL), o_mla_s, o_sb_s, mods_s[2:], wts,
                                   tm=tm_s, per_token_mod=True)

    tm_r = tok_tile
    idx, wgt, rank, counts = _route(h2_p, h2_s, wts["w_r_hi"], wts["w_r_lo"], wts["b_router"], tm_r)
    counts = counts.reshape(N_EXPERTS).astype(I32)
    padded = (counts + EXPERT_BLOCK - 1) // EXPERT_BLOCK * EXPERT_BLOCK
    pend = jnp.cumsum(padded)
    pstart = (pend - padded).astype(I32)
    n_assign = t_all * TOP_K
    n_rows = -(-n_assign // EXPERT_BLOCK) * EXPERT_BLOCK + N_EXPERTS * EXPERT_BLOCK
    n_blocks = n_rows // EXPERT_BLOCK
    block_start = jnp.arange(n_blocks, dtype=I32) * EXPERT_BLOCK
    block_e = jnp.minimum(jnp.sum((pend[None, :] <= block_start[:, None]).astype(I32), axis=1), N_EXPERTS - 1)
    n_used = (pend[-1:] // EXPERT_BLOCK).astype(I32)
    dest = jnp.sum(jnp.where(idx[:, :, None] == jnp.arange(N_EXPERTS, dtype=I32), pstart, 0), axis=-1) + rank
    xs_rows = _dispatch(dest, h2_p, h2_s, n_rows, tm_r)
    ys_rows = _experts(l, block_e, n_used, xs_rows, w_exp_gate, w_exp_up, w_exp_down)
    y_p, y_s = _combine(dest, wgt.T, ys_rows, base_p, base_s, mods_p[5],
                        mods_s[5].reshape(t_s, D_MODEL), tm_r, seq)
    y_p = y_p.reshape(batch, seq, D_MODEL)
    y_s = y_s.reshape(nb, 1, D_MODEL)

    new_p = (ckv_p.reshape(batch, seq, KV_LORA), kpe_p[:, NOPE:QK].reshape(batch, seq, ROPE),
             kinv_p[:, :N_HEADS].reshape(batch, seq, N_HEADS),
             sbk_p.reshape(batch, N_HEADS, SB_DIM, seq).transpose(0, 3, 1, 2),
             sbv_p.reshape(batch, N_HEADS, SB_DIM, seq).transpose(0, 3, 1, 2))
    new_s = (ckv_s.reshape(nb, 1, KV_LORA), kpe_s[:, NOPE:QK].reshape(nb, 1, ROPE),
             kinv_s[:, :N_HEADS].reshape(nb, 1, N_HEADS),
             sbk_s.reshape(nb, 1, N_HEADS, SB_DIM), sbv_s.reshape(nb, 1, N_HEADS, SB_DIM))
    return y_p, y_s, new_p, new_s


def kernel(x_prompt, x_sample, cache_mla_ckv, cache_mla_kpe, cache_mla_kinv, cache_sb_k, cache_sb_v, page_table, c_prompt, c_sample, w_ada, b_ada, g_attn, g_moe, w_in, g_q_lat, w_uq, g_kv_lat, w_uk, w_uv, g_qk_q, g_qk_k, w_o, w_router, b_router, w_exp_gate, w_exp_up, w_exp_down, w_sh_gate, w_sh_up, w_sh_down):
    depth = w_in.shape[0]
    y_p, y_s = x_prompt, x_sample
    p_rows = [[] for _ in range(5)]
    s_rows = [[] for _ in range(5)]
    for l in range(depth):
        wts = _layer_weights(l, w_ada, b_ada, g_attn, g_moe, w_in, g_q_lat, w_uq, g_kv_lat, w_uk, w_uv,
                             g_qk_q, g_qk_k, w_o, w_router, b_router, w_sh_gate, w_sh_up, w_sh_down)
        y_p, y_s, new_p, new_s = _layer(l, y_p, y_s, cache_mla_ckv, cache_mla_kpe, cache_mla_kinv,
                                        cache_sb_k, cache_sb_v, page_table, c_prompt, c_sample, wts,
                                        w_exp_gate, w_exp_up, w_exp_down)
        for lst, a in zip(p_rows, new_p):
            lst.append(a)
        for lst, a in zip(s_rows, new_s):
            lst.append(a)
    outs_p = [jnp.stack(r) for r in p_rows]
    outs_s = [jnp.stack(r) for r in s_rows]
    return (y_p, y_s, *outs_p, *outs_s)
```
